```python
import math
import jax, jax.numpy as jnp
from jax import lax
import numpy as np

D_MODEL = 1024
BATCH = 8
SEQ = 8192
DEPTH = 2

CHUNK = 64
Q_BLOCK = 128
EPS = 1e-6

MLA_HEADS = 8
QK_NOPE_DIM = 64
QK_ROPE_DIM = 32
V_HEAD_DIM = 64
Q_LORA_RANK = 768
KV_LORA_RANK = 256
ROPE_THETA = 10000.0
MLA_WIDTH = MLA_HEADS * V_HEAD_DIM

SSD_HEADS = 8
SSD_HEAD_DIM = 64
SSD_INNER = SSD_HEADS * SSD_HEAD_DIM
SSD_GROUPS = 2
SSD_STATE = 128
CONV_WIDTH = 4
CONV_DIM = SSD_INNER + 2 * SSD_GROUPS * SSD_STATE

MIX_WIDTH = MLA_WIDTH + SSD_INNER
D_FF = 4 * D_MODEL
IN_PROJ_DIM = Q_LORA_RANK + KV_LORA_RANK + QK_ROPE_DIM + SSD_INNER + CONV_DIM + SSD_HEADS

kernel_name = "hymba_mla_ssd_sandwich_trunk"


def rms_norm(x, w):
    xf = x.astype(jnp.float32)
    xf = xf * lax.rsqrt(jnp.mean(xf * xf, axis=-1, keepdims=True) + EPS)
    return xf.astype(x.dtype) * w


def rope_tables(positions):
    inv_freq = ROPE_THETA ** (-jnp.arange(0, QK_ROPE_DIM, 2, dtype=jnp.float32) / QK_ROPE_DIM)
    ang = positions[..., None].astype(jnp.float32) * inv_freq
    return jnp.cos(ang), jnp.sin(ang)


def apply_rope(t, cos, sin):
    half = t.shape[-1] // 2
    t1 = t[..., :half].astype(jnp.float32)
    t2 = t[..., half:].astype(jnp.float32)
    return jnp.concatenate([t1 * cos - t2 * sin, t2 * cos + t1 * sin], axis=-1).astype(t.dtype)


def mla_mixer(c_q, c_kv, k_rope, cos, sin, q_norm_w, w_uq, kv_norm_w, w_ukv):
    B, S, _ = c_q.shape
    q = (rms_norm(c_q, q_norm_w) @ w_uq).reshape(B, S, MLA_HEADS, QK_NOPE_DIM + QK_ROPE_DIM)
    q_nope = q[..., :QK_NOPE_DIM]
    q_rope = apply_rope(q[..., QK_NOPE_DIM:], cos[:, :, None], sin[:, :, None])
    k_rope = apply_rope(k_rope, cos, sin)
    kv = (rms_norm(c_kv, kv_norm_w) @ w_ukv).reshape(B, S, MLA_HEADS, QK_NOPE_DIM + V_HEAD_DIM)
    k_nope = kv[..., :QK_NOPE_DIM]
    v = kv[..., QK_NOPE_DIM:]
    scale = (QK_NOPE_DIM + QK_ROPE_DIM) ** -0.5
    nb = S // Q_BLOCK
    key_chunk = jnp.arange(S) // CHUNK

    def to_blocks(t):
        return jnp.moveaxis(t.reshape(B, nb, Q_BLOCK, *t.shape[2:]), 1, 0)

    def attend(args):
        blk, qn, qr = args
        s = (jnp.einsum("bqhd,bkhd->bhqk", qn, k_nope, preferred_element_type=jnp.float32)
             + jnp.einsum("bqhr,bkr->bhqk", qr, k_rope, preferred_element_type=jnp.float32)) * scale
        q_chunk = (blk * Q_BLOCK + jnp.arange(Q_BLOCK)) // CHUNK
        allowed = key_chunk[None, :] <= q_chunk[:, None]
        s = jnp.where(allowed[None, None], s, -jnp.inf)
        p = jax.nn.softmax(s, axis=-1).astype(v.dtype)
        return jnp.einsum("bhqk,bkhd->bqhd", p, v)

    o = lax.map(attend, (jnp.arange(nb), to_blocks(q_nope), to_blocks(q_rope)))
    return jnp.moveaxis(o, 0, 1).reshape(B, S, MLA_WIDTH)


def causal_depthwise_conv(x, w, b):
    S = x.shape[1]
    xp = jnp.pad(x, ((0, 0), (CONV_WIDTH - 1, 0), (0, 0)))
    y = xp[:, 0:S] * w[0]
    for k in range(1, CONV_WIDTH):
        y = y + xp[:, k:k + S] * w[k]
    return y + b


def segsum(a):
    L = a.shape[-1]
    cs = jnp.cumsum(a, axis=-1)
    diff = cs[..., :, None] - cs[..., None, :]
    tril = jnp.tril(jnp.ones((L, L), dtype=bool))
    return jnp.where(tril, diff, -jnp.inf)


def ssd_chunked(x, dt, A, Bm, Cm):
    Bsz, S, H, P = x.shape
    nc = S // CHUNK
    rep = H // SSD_GROUPS
    xdt = (x * dt[..., None]).reshape(Bsz, nc, CHUNK, H, P)
    a = jnp.transpose((dt * A).reshape(Bsz, nc, CHUNK, H), (0, 3, 1, 2))
    Bh = jnp.repeat(Bm, rep, axis=2).reshape(Bsz, nc, CHUNK, H, SSD_STATE)
    Ch = jnp.repeat(Cm, rep, axis=2).reshape(Bsz, nc, CHUNK, H, SSD_STATE)
    a_cs = jnp.cumsum(a, axis=-1)
    decay_in = jnp.exp(segsum(a))
    scores = jnp.einsum("bclhn,bcshn->bhcls", Ch, Bh) * decay_in
    y_diag = jnp.einsum("bhcls,bcshp->bclhp", scores, xdt)
    decay_states = jnp.exp(a_cs[..., -1:] - a_cs)
    states = jnp.einsum("bcshn,bhcs,bcshp->bchpn", Bh, decay_states, xdt)
    chunk_decay = jnp.exp(a_cs[..., -1])

    def step(h, inp):
        st, dec = inp
        return h * dec[..., None, None] + st, h

    init = jnp.zeros((Bsz, H, P, SSD_STATE), dtype=x.dtype)
    _, prev = lax.scan(step, init, (jnp.moveaxis(states, 1, 0), jnp.moveaxis(chunk_decay, 2, 0)))
    prev = jnp.moveaxis(prev, 0, 1)
    y_off = jnp.einsum("bclhn,bchpn,bhcl->bclhp", Ch, prev, jnp.exp(a_cs))
    return (y_diag + y_off).reshape(Bsz, S, H, P)


def ssd_mixer(z, xbc, dt_raw, conv_w, conv_b, dt_bias, a_log, d_skip, ssd_norm_w):
    B, S, _ = z.shape
    out_dtype = z.dtype
    xbc = jax.nn.silu(causal_depthwise_conv(xbc, conv_w, conv_b)).astype(jnp.float32)
    xs = xbc[..., :SSD_INNER].reshape(B, S, SSD_HEADS, SSD_HEAD_DIM)
    Bm = xbc[..., SSD_INNER:SSD_INNER + SSD_GROUPS * SSD_STATE].reshape(B, S, SSD_GROUPS, SSD_STATE)
    Cm = xbc[..., SSD_INNER + SSD_GROUPS * SSD_STATE:].reshape(B, S, SSD_GROUPS, SSD_STATE)
    dt = jax.nn.softplus(dt_raw.astype(jnp.float32) + dt_bias.astype(jnp.float32))
    A = -jnp.exp(a_log.astype(jnp.float32))
    y = ssd_chunked(xs, dt, A, Bm, Cm) + d_skip.astype(jnp.float32)[:, None] * xs
    y = y.reshape(B, S, SSD_INNER) * jax.nn.silu(z.astype(jnp.float32))
    yg = y.reshape(B, S, SSD_GROUPS, SSD_INNER // SSD_GROUPS)
    yg = yg * lax.rsqrt(jnp.mean(yg * yg, axis=-1, keepdims=True) + EPS)
    return (yg.reshape(B, S, SSD_INNER) * ssd_norm_w.astype(jnp.float32)).astype(out_dtype)


def setup_inputs(seed: int = 0) -> dict:
    key = jax.random.key(seed)
    ks = jax.random.split(key, 24)
    f32 = jnp.float32

    def normal(k, shape, scale):
        return jax.random.normal(k, shape, f32) * scale

    def gain(k, shape):
        return 1.0 + 0.02 * jax.random.normal(k, shape, f32)

    x = jax.random.normal(ks[0], (BATCH, SEQ, D_MODEL), f32)
    offset = jax.random.randint(ks[1], (BATCH, 1), 0, 4096, dtype=jnp.int32)
    positions = (offset + jnp.arange(SEQ, dtype=jnp.int32)[None, :]).astype(jnp.int32)
    dt0 = jnp.exp(jax.random.uniform(ks[2], (DEPTH, SSD_HEADS), f32)
                  * (math.log(0.1) - math.log(0.001)) + math.log(0.001))
    dt_bias = dt0 + jnp.log(-jnp.expm1(-dt0))
    a_log = jnp.log(jax.random.uniform(ks[3], (DEPTH, SSD_HEADS), f32, minval=1.0, maxval=16.0))
    return {
        "x": x,
        "positions": positions,
        "pre_mix_norm": gain(ks[4], (DEPTH, D_MODEL)),
        "w_in": normal(ks[5], (DEPTH, D_MODEL, IN_PROJ_DIM), D_MODEL ** -0.5),
        "q_norm": gain(ks[6], (DEPTH, Q_LORA_RANK)),
        "w_uq": normal(ks[7], (DEPTH, Q_LORA_RANK, MLA_HEADS * (QK_NOPE_DIM + QK_ROPE_DIM)), Q_LORA_RANK ** -0.5),
        "kv_norm": gain(ks[8], (DEPTH, KV_LORA_RANK)),
        "w_ukv": normal(ks[9], (DEPTH, KV_LORA_RANK, MLA_HEADS * (QK_NOPE_DIM + V_HEAD_DIM)), KV_LORA_RANK ** -0.5),
        "conv_w": normal(ks[10], (DEPTH, CONV_WIDTH, CONV_DIM), CONV_WIDTH ** -0.5),
        "conv_b": normal(ks[11], (DEPTH, CONV_DIM), 0.01),
        "dt_bias": dt_bias,
        "a_log": a_log,
        "d_skip": gain(ks[12], (DEPTH, SSD_HEADS)),
        "ssd_norm": gain(ks[13], (DEPTH, SSD_INNER)),
        "w_out": normal(ks[14], (DEPTH, MIX_WIDTH, D_MODEL), MIX_WIDTH ** -0.5),
        "post_mix_norm": gain(ks[15], (DEPTH, D_MODEL)),
        "pre_mlp_norm": gain(ks[16], (DEPTH, D_MODEL)),
        "w_up": normal(ks[17], (DEPTH, D_MODEL, D_FF), D_MODEL ** -0.5),
        "w_down": normal(ks[18], (DEPTH, D_FF, D_MODEL), D_FF ** -0.5),
        "post_mlp_norm": gain(ks[19], (DEPTH, D_MODEL)),
    }


def reference(x, positions, pre_mix_norm, w_in, q_norm, w_uq, kv_norm, w_ukv, conv_w, conv_b,
              dt_bias, a_log, d_skip, ssd_norm, w_out, post_mix_norm, pre_mlp_norm, w_up,
              w_down, post_mlp_norm):
    cos, sin = rope_tables(positions)
    s1 = Q_LORA_RANK
    s2 = s1 + KV_LORA_RANK
    s3 = s2 + QK_ROPE_DIM
    s4 = s3 + SSD_INNER
    s5 = s4 + CONV_DIM
    h = x
    for l in range(DEPTH):
        u = rms_norm(h, pre_mix_norm[l])
        proj = u @ w_in[l]
        c_q, c_kv, k_rope = proj[..., :s1], proj[..., s1:s2], proj[..., s2:s3]
        z, xbc, dt_raw = proj[..., s3:s4], proj[..., s4:s5], proj[..., s5:]
        y_att = mla_mixer(c_q, c_kv, k_rope, cos, sin, q_norm[l], w_uq[l], kv_norm[l], w_ukv[l])
        y_ssd = ssd_mixer(z, xbc, dt_raw, conv_w[l], conv_b[l], dt_bias[l], a_log[l], d_skip[l], ssd_norm[l])
        mixed = jnp.concatenate([y_att, y_ssd], axis=-1) @ w_out[l]
        h = h + rms_norm(mixed, post_mix_norm[l])
        m = rms_norm(h, pre_mlp_norm[l])
        m = jnp.square(jax.nn.relu(m @ w_up[l])) @ w_down[l]
        h = h + rms_norm(m, post_mlp_norm[l])
    return h
```

```python
import functools
import math

import jax
import jax.numpy as jnp
from jax import lax
from jax.experimental import pallas as pl
from jax.experimental.pallas import tpu as pltpu

F32 = jnp.float32
BF16 = jnp.bfloat16

D_MODEL = 1024
CHUNK = 64
EPS = 1e-6
MLA_HEADS = 8
QK_NOPE_DIM = 64
QK_ROPE_DIM = 32
V_HEAD_DIM = 64
Q_LORA_RANK = 768
KV_LORA_RANK = 256
ROPE_THETA = 10000.0
MLA_WIDTH = MLA_HEADS * V_HEAD_DIM
SSD_HEADS = 8
SSD_HEAD_DIM = 64
SSD_INNER = SSD_HEADS * SSD_HEAD_DIM
SSD_GROUPS = 2
SSD_STATE = 128
CONV_WIDTH = 4
CONV_DIM = SSD_INNER + 2 * SSD_GROUPS * SSD_STATE
D_FF = 4 * D_MODEL

LANES = 128
SUBLANES = 8
HEAD_PAD = LANES
HALF_ROPE = QK_ROPE_DIM // 2
ROPE_LO = QK_NOPE_DIM
ROPE_MID = QK_NOPE_DIM + HALF_ROPE
ROPE_HI = QK_NOPE_DIM + QK_ROPE_DIM
HEADS_PER_GROUP = SSD_HEADS // SSD_GROUPS
GROUP_WIDTH = HEADS_PER_GROUP * SSD_HEAD_DIM

C_Q0, C_Q1 = 0, Q_LORA_RANK
C_KV0, C_KV1 = C_Q1, C_Q1 + KV_LORA_RANK
C_KR0, C_KR1 = C_KV1, C_KV1 + HEAD_PAD
C_Z0, C_Z1 = C_KR1, C_KR1 + SSD_INNER
C_X0, C_X1 = C_Z1, C_Z1 + CONV_DIM
C_DT0, C_DT1 = C_X1, C_X1 + SSD_INNER
PROJ_COLS = C_DT1

PROJ_ROWS = 512
ATTN_ROWS = 512
SSD_ROWS = 512
MLP_ROWS = 512
FF_BLOCK = 1024
VMEM_LIMIT = 56 * 1024 * 1024


def _rms(x, w):
    return x * lax.rsqrt(jnp.mean(x * x, axis=-1, keepdims=True) + EPS) * w


def _const_spec(shape):
    zeros = (0,) * len(shape)
    return pl.BlockSpec(shape, lambda *_: zeros, pipeline_mode=pl.Buffered(1))


def _proj_body(h_ref, pos_ref, invf_ref, prew_ref, w1_ref, qnw_ref, wuq_ref, kvnw_ref, wuk_ref, wuv_ref,
               q_ref, k_ref, v_ref, z_ref, xbc_ref, dt_ref):
    u = _rms(h_ref[...], prew_ref[...]).astype(BF16)

    def seg(c0, c1):
        return jnp.dot(u, w1_ref[:, c0:c1], preferred_element_type=F32)

    c_q = seg(C_Q0, C_Q1)
    c_kv = seg(C_KV0, C_KV1)
    k_rope = seg(C_KR0, C_KR1)
    z_ref[...] = seg(C_Z0, C_Z1)
    xbc_ref[...] = seg(C_X0, C_X1)
    dt_ref[...] = seg(C_DT0, C_DT1)

    lane = lax.broadcasted_iota(jnp.int32, (1, LANES), 1)
    first_half = (lane >= ROPE_LO) & (lane < ROPE_MID)
    second_half = (lane >= ROPE_MID) & (lane < ROPE_HI)
    ang = pos_ref[...].astype(F32) * invf_ref[...]
    cos = jnp.where(first_half | second_half, jnp.cos(ang), 1.0)
    sin = jnp.sin(ang)
    sin_first = jnp.where(first_half, -sin, 0.0)
    sin_second = jnp.where(second_half, sin, 0.0)

    def rope(t):
        return (t * cos + pltpu.roll(t, LANES - HALF_ROPE, 1) * sin_first
                + pltpu.roll(t, HALF_ROPE, 1) * sin_second)

    scale = (QK_NOPE_DIM + QK_ROPE_DIM) ** -0.5
    qf = jnp.dot(_rms(c_q, qnw_ref[...]).astype(BF16), wuq_ref[...], preferred_element_type=F32)
    c_kv_n = _rms(c_kv, kvnw_ref[...]).astype(BF16)
    kf = jnp.dot(c_kv_n, wuk_ref[...], preferred_element_type=F32)
    k_rope = rope(k_rope)
    for h in range(MLA_HEADS):
        blk = slice(h * HEAD_PAD, (h + 1) * HEAD_PAD)
        q_ref[h] = (rope(qf[:, blk]) * scale).astype(BF16)
        k_ref[h] = (kf[:, blk] + k_rope).astype(BF16)
    v_ref[...] = jnp.dot(c_kv_n, wuv_ref[...], preferred_element_type=F32).astype(BF16)


def _proj_call(h, pos_b, invf, prew, w1, qnw, wuq, kvnw, wuk, wuv):
    T = h.shape[0]
    rows = PROJ_ROWS
    row_spec = lambda cols: pl.BlockSpec((rows, cols), lambda i: (i, 0))
    head_spec = pl.BlockSpec((MLA_HEADS, rows, HEAD_PAD), lambda i: (0, i, 0))
    return pl.pallas_call(
        _proj_body,
        grid=(T // rows,),
        in_specs=[row_spec(D_MODEL), row_spec(LANES), _const_spec((1, LANES)), _const_spec((1, D_MODEL)),
                  _const_spec((D_MODEL, PROJ_COLS)), _const_spec((1, Q_LORA_RANK)),
                  _const_spec((Q_LORA_RANK, MLA_HEADS * HEAD_PAD)), _const_spec((1, KV_LORA_RANK)),
                  _const_spec((KV_LORA_RANK, MLA_HEADS * HEAD_PAD)), _const_spec((KV_LORA_RANK, MLA_WIDTH))],
        out_specs=[head_spec, head_spec, row_spec(MLA_WIDTH), row_spec(SSD_INNER), row_spec(CONV_DIM),
                   row_spec(SSD_INNER)],
        out_shape=[jax.ShapeDtypeStruct((MLA_HEADS, T, HEAD_PAD), BF16),
                   jax.ShapeDtypeStruct((MLA_HEADS, T, HEAD_PAD), BF16),
                   jax.ShapeDtypeStruct((T, MLA_WIDTH), BF16),
                   jax.ShapeDtypeStruct((T, SSD_INNER), F32),
                   jax.ShapeDtypeStruct((T, CONV_DIM), F32),
                   jax.ShapeDtypeStruct((T, SSD_INNER), F32)],
        compiler_params=pltpu.CompilerParams(dimension_semantics=("parallel",), vmem_limit_bytes=VMEM_LIMIT),
        name="proj",
    )(h, pos_b, invf, prew, w1, qnw, wuq, kvnw, wuk, wuv)


def _attn_body(q_ref, k_ref, v_ref, o_ref, m_scr, l_scr, acc_scr, *, rows):
    qi = pl.program_id(2)
    lane = lax.broadcasted_iota(jnp.int32, (1, LANES), 1)
    low_head = lane < V_HEAD_DIM

    def kv_tile(j, diagonal):
        start = pl.multiple_of(j * rows, rows)
        v = v_ref[pl.ds(start, rows), :]
        v_split = (jnp.where(low_head, v, jnp.zeros_like(v)), jnp.where(low_head, jnp.zeros_like(v), v))
        pv = None
        alphas = []
        for hh in range(2):
            k = k_ref[hh, pl.ds(start, rows), :]
            s = lax.dot_general(q_ref[hh], k, (((1,), (1,)), ((), ())), preferred_element_type=F32)
            if diagonal:
                q_chunk = lax.broadcasted_iota(jnp.int32, (rows, rows), 0) // CHUNK
                k_chunk = lax.broadcasted_iota(jnp.int32, (rows, rows), 1) // CHUNK
                s = jnp.where(k_chunk <= q_chunk, s, -jnp.inf)
            m_cur = jnp.max(s, axis=1, keepdims=True)
            if diagonal:
                m_new = m_cur
            else:
                m_prev = m_scr[hh][:, :1]
                m_new = jnp.maximum(m_prev, m_cur)
            p = jnp.exp(s - m_new)
            l_cur = jnp.sum(p, axis=1, keepdims=True)
            if diagonal:
                l_new = l_cur
            else:
                alpha = jnp.exp(m_prev - m_new)
                alphas.append(alpha)
                l_new = alpha * l_scr[hh][:, :1] + l_cur
            m_scr[hh] = jnp.broadcast_to(m_new, (rows, LANES))
            l_scr[hh] = jnp.broadcast_to(l_new, (rows, LANES))
            contrib = jnp.dot(p.astype(BF16), v_split[hh], preferred_element_type=F32)
            pv = contrib if pv is None else pv + contrib
        if diagonal:
            acc_scr[...] = pv
        else:
            acc_scr[...] = acc_scr[...] * jnp.where(low_head, alphas[0], alphas[1]) + pv

    kv_tile(qi, True)

    def body(j, carry):
        kv_tile(j, False)
        return carry

    lax.fori_loop(0, qi, body, 0)
    denom = jnp.where(low_head, l_scr[0], l_scr[1])
    o_ref[...] = (acc_scr[...] / denom).astype(o_ref.dtype)


def _attn_call(q, k, v, batch, seq):
    rows = ATTN_ROWS
    n_q = seq // rows
    pairs = MLA_HEADS // 2
    return pl.pallas_call(
        functools.partial(_attn_body, rows=rows),
        grid=(batch, pairs, n_q),
        in_specs=[pl.BlockSpec((2, rows, HEAD_PAD), lambda b, p, i: (p, b * n_q + i, 0)),
                  pl.BlockSpec((2, seq, HEAD_PAD), lambda b, p, i: (p, b, 0)),
                  pl.BlockSpec((seq, 2 * V_HEAD_DIM), lambda b, p, i: (b, p))],
        out_specs=pl.BlockSpec((rows, 2 * V_HEAD_DIM), lambda b, p, i: (b * n_q + i, p)),
        out_shape=jax.ShapeDtypeStruct((batch * seq, MLA_WIDTH), BF16),
        scratch_shapes=[pltpu.VMEM((2, rows, LANES), F32), pltpu.VMEM((2, rows, LANES), F32),
                        pltpu.VMEM((rows, 2 * V_HEAD_DIM), F32)],
        compiler_params=pltpu.CompilerParams(dimension_semantics=("parallel", "parallel", "arbitrary"),
                                             vmem_limit_bytes=VMEM_LIMIT),
        name="attn",
    )(q, k, v)


def _ssd_body(z_ref, xbc_ref, dt_ref, convw_ref, convb_ref, dtb_ref, alog_ref, dskip_ref, normw_ref,
              o_ref, ext_scr, act_scr, dts_scr, y_scr, state_scr, *, rows):
    tail = SUBLANES
    L = CHUNK

    @pl.when(pl.program_id(1) == 0)
    def _():
        ext_scr[0:tail, :] = jnp.zeros((tail, CONV_DIM), F32)
        state_scr[...] = jnp.zeros_like(state_scr)

    ext_scr[tail:tail + rows, :] = xbc_ref[...]
    conv = ext_scr[tail:tail + rows, :] * convw_ref[CONV_WIDTH - 1:CONV_WIDTH, :]
    for back in range(1, CONV_WIDTH):
        w_row = convw_ref[CONV_WIDTH - 1 - back:CONV_WIDTH - back, :]
        conv = conv + ext_scr[tail - back:tail - back + rows, :] * w_row
    conv = conv + convb_ref[...]
    act_scr[...] = conv * jax.nn.sigmoid(conv)
    ext_scr[0:tail, :] = ext_scr[rows:rows + tail, :]
    dts_scr[...] = jax.nn.softplus(dt_ref[...] + dtb_ref[...])

    a_coef = -jnp.exp(alog_ref[...])
    row_i = lax.broadcasted_iota(jnp.int32, (L, SSD_INNER), 0)
    col_j = lax.broadcasted_iota(jnp.int32, (L, SSD_INNER), 1) % SSD_HEAD_DIM
    upper = (row_i <= col_j).astype(F32)
    lower = col_j <= row_i
    tril = (lax.broadcasted_iota(jnp.int32, (L, L), 1) <= lax.broadcasted_iota(jnp.int32, (L, L), 0)).astype(F32)
    blk_r = lax.broadcasted_iota(jnp.int32, (GROUP_WIDTH, GROUP_WIDTH), 0) // SSD_HEAD_DIM
    blk_c = lax.broadcasted_iota(jnp.int32, (GROUP_WIDTH, GROUP_WIDTH), 1) // SSD_HEAD_DIM
    same_head = blk_r == blk_c

    def chunk(c, carry):
        r0 = pl.multiple_of(c * L, L)
        xs = act_scr[pl.ds(r0, L), 0:SSD_INNER]
        dt = dts_scr[pl.ds(r0, L), :]
        a = dt * a_coef
        xdt = xs * dt
        a_cs = jnp.dot(tril, a, preferred_element_type=F32, precision=lax.Precision.HIGHEST)
        a_cs_row = jnp.sum(a * upper, axis=0, keepdims=True)
        decay_in = jnp.exp(jnp.where(lower, a_cs - a_cs_row, -jnp.inf))
        a_last = a_cs[L - 1:L, :]
        decay_to_end = jnp.exp(a_last - a_cs)
        decay_from_start = jnp.exp(a_cs)
        chunk_decay = jnp.exp(a_last)
        for g in range(SSD_GROUPS):
            gl = slice(g * GROUP_WIDTH, (g + 1) * GROUP_WIDTH)
            b_off = SSD_INNER + g * SSD_STATE
            c_off = SSD_INNER + SSD_GROUPS * SSD_STATE + g * SSD_STATE
            b_f32 = act_scr[pl.ds(r0, L), b_off:b_off + SSD_STATE]
            b_g = b_f32.astype(BF16)
            c_g = act_scr[pl.ds(r0, L), c_off:c_off + SSD_STATE].astype(BF16)
            b_rep = jnp.concatenate([b_g] * HEADS_PER_GROUP, axis=0)
            cb = lax.dot_general(c_g, b_rep, (((1,), (1,)), ((), ())), preferred_element_type=F32)
            scores = (cb * decay_in[:, gl]).astype(BF16)
            x_g = xdt[:, gl]
            x_rep = jnp.concatenate([x_g] * HEADS_PER_GROUP, axis=0)
            x_diag = jnp.where(same_head, x_rep, 0.0).astype(BF16)
            y_diag = jnp.dot(scores, x_diag, preferred_element_type=F32)
            state = state_scr[g]
            y_off = jnp.dot(c_g, state.astype(BF16), preferred_element_type=F32) * decay_from_start[:, gl]
            x_dec = (x_g * decay_to_end[:, gl]).astype(BF16)
            new_state = jnp.dot(b_f32.T.astype(BF16), x_dec, preferred_element_type=F32)
            state_scr[g] = state * chunk_decay[:, gl] + new_state
            y_scr[pl.ds(r0, L), gl] = y_diag + y_off + dskip_ref[:, gl] * xs[:, gl]
        return carry

    lax.fori_loop(0, rows // L, chunk, 0)

    zf = z_ref[...]
    y = y_scr[...] * (zf * jax.nn.sigmoid(zf))
    for g in range(SSD_GROUPS):
        gl = slice(g * GROUP_WIDTH, (g + 1) * GROUP_WIDTH)
        yg = y[:, gl]
        yg = yg * lax.rsqrt(jnp.mean(yg * yg, axis=-1, keepdims=True) + EPS)
        o_ref[:, gl] = (yg * normw_ref[:, gl]).astype(o_ref.dtype)


def _ssd_call(z, xbc, dt, convw, convb, dtb, alog, dskip, normw, batch, seq):
    rows = SSD_ROWS
    n_t = seq // rows
    row_spec = lambda cols: pl.BlockSpec((rows, cols), lambda b, i: (b * n_t + i, 0))
    return pl.pallas_call(
        functools.partial(_ssd_body, rows=rows),
        grid=(batch, n_t),
        in_specs=[row_spec(SSD_INNER), row_spec(CONV_DIM), row_spec(SSD_INNER),
                  _const_spec((CONV_WIDTH, CONV_DIM)), _const_spec((1, CONV_DIM)), _const_spec((1, SSD_INNER)),
                  _const_spec((1, SSD_INNER)), _const_spec((1, SSD_INNER)), _const_spec((1, SSD_INNER))],
        out_specs=row_spec(SSD_INNER),
        out_shape=jax.ShapeDtypeStruct((batch * seq, SSD_INNER), BF16),
        scratch_shapes=[pltpu.VMEM((rows + SUBLANES, CONV_DIM), F32), pltpu.VMEM((rows, CONV_DIM), F32),
                        pltpu.VMEM((rows, SSD_INNER), F32), pltpu.VMEM((rows, SSD_INNER), F32),
                        pltpu.VMEM((SSD_GROUPS, SSD_STATE, GROUP_WIDTH), F32)],
        compiler_params=pltpu.CompilerParams(dimension_semantics=("arbitrary", "arbitrary"),
                                             vmem_limit_bytes=VMEM_LIMIT),
        name="ssd",
    )(z, xbc, dt, convw, convb, dtb, alog, dskip, normw)


def _mlp_body(h_ref, ya_ref, ys_ref, wout_ref, postmix_ref, premlp_ref, wup_ref, wdn_ref, postmlp_ref, o_ref):
    mixed = (jnp.dot(ya_ref[...], wout_ref[0:MLA_WIDTH, :], preferred_element_type=F32)
             + jnp.dot(ys_ref[...], wout_ref[MLA_WIDTH:, :], preferred_element_type=F32))
    h1 = h_ref[...] + _rms(mixed, postmix_ref[...])
    m = _rms(h1, premlp_ref[...]).astype(BF16)
    acc = None
    for c in range(D_FF // FF_BLOCK):
        ff = slice(c * FF_BLOCK, (c + 1) * FF_BLOCK)
        up = jnp.dot(m, wup_ref[:, ff], preferred_element_type=F32)
        act = jnp.square(jnp.maximum(up, 0.0)).astype(BF16)
        part = jnp.dot(act, wdn_ref[ff, :], preferred_element_type=F32)
        acc = part if acc is None else acc + part
    o_ref[...] = h1 + _rms(acc, postmlp_ref[...])


def _mlp_call(h, ya, ys, wout, postmix, premlp, wup, wdn, postmlp):
    T = h.shape[0]
    rows = MLP_ROWS
    row_spec = lambda cols: pl.BlockSpec((rows, cols), lambda i: (i, 0))
    return pl.pallas_call(
        _mlp_body,
        grid=(T // rows,),
        in_specs=[row_spec(D_MODEL), row_spec(MLA_WIDTH), row_spec(SSD_INNER),
                  _const_spec((MLA_WIDTH + SSD_INNER, D_MODEL)), _const_spec((1, D_MODEL)),
                  _const_spec((1, D_MODEL)), _const_spec((D_MODEL, D_FF)), _const_spec((D_FF, D_MODEL)),
                  _const_spec((1, D_MODEL))],
        out_specs=row_spec(D_MODEL),
        out_shape=jax.ShapeDtypeStruct((T, D_MODEL), F32),
        compiler_params=pltpu.CompilerParams(dimension_semantics=("parallel",), vmem_limit_bytes=VMEM_LIMIT),
        name="mlp",
    )(h, ya, ys, wout, postmix, premlp, wup, wdn, postmlp)


def _pack_w_in(w_in):
    s1 = Q_LORA_RANK
    s2 = s1 + KV_LORA_RANK
    s3 = s2 + QK_ROPE_DIM
    s4 = s3 + SSD_INNER
    s5 = s4 + CONV_DIM
    zeros = lambda n: jnp.zeros((D_MODEL, n), w_in.dtype)
    k_rope = jnp.concatenate([zeros(ROPE_LO), w_in[:, s2:s3], zeros(HEAD_PAD - ROPE_HI)], axis=1)
    dt_rep = jnp.repeat(w_in[:, s5:], SSD_HEAD_DIM, axis=1)
    return jnp.concatenate([w_in[:, :s2], k_rope, w_in[:, s3:s5], dt_rep], axis=1).astype(BF16)


def _pack_w_uq(w_uq):
    w = w_uq.reshape(Q_LORA_RANK, MLA_HEADS, QK_NOPE_DIM + QK_ROPE_DIM)
    pad = jnp.zeros((Q_LORA_RANK, MLA_HEADS, HEAD_PAD - ROPE_HI), w_uq.dtype)
    return jnp.concatenate([w, pad], axis=-1).reshape(Q_LORA_RANK, MLA_HEADS * HEAD_PAD).astype(BF16)


def _pack_w_ukv(w_ukv):
    w = w_ukv.reshape(KV_LORA_RANK, MLA_HEADS, QK_NOPE_DIM + V_HEAD_DIM)
    pad = jnp.zeros((KV_LORA_RANK, MLA_HEADS, HEAD_PAD - QK_NOPE_DIM), w_ukv.dtype)
    w_uk = jnp.concatenate([w[..., :QK_NOPE_DIM], pad], axis=-1).reshape(KV_LORA_RANK, MLA_HEADS * HEAD_PAD)
    w_uv = w[..., QK_NOPE_DIM:].reshape(KV_LORA_RANK, MLA_WIDTH)
    return w_uk.astype(BF16), w_uv.astype(BF16)


def _per_lane(v):
    return jnp.repeat(v.astype(F32), SSD_HEAD_DIM)[None, :]


def kernel(x, positions, pre_mix_norm, w_in, q_norm, w_uq, kv_norm, w_ukv, conv_w, conv_b, dt_bias, a_log,
           d_skip, ssd_norm, w_out, post_mix_norm, pre_mlp_norm, w_up, w_down, post_mlp_norm):
    batch, seq, _ = x.shape
    depth = w_in.shape[0]
    T = batch * seq
    assert seq % ATTN_ROWS == 0 and seq % SSD_ROWS == 0 and T % PROJ_ROWS == 0 and T % MLP_ROWS == 0

    inv_freq = ROPE_THETA ** (-jnp.arange(0, QK_ROPE_DIM, 2, dtype=F32) / QK_ROPE_DIM)
    invf = jnp.concatenate([jnp.zeros((ROPE_LO,), F32), inv_freq, inv_freq,
                            jnp.zeros((HEAD_PAD - ROPE_HI,), F32)])[None, :]
    pos_b = jnp.broadcast_to(positions.reshape(T, 1), (T, LANES))
    row = lambda v: v.astype(F32)[None, :]

    h = x.reshape(T, D_MODEL)
    for l in range(depth):
        w_uk, w_uv = _pack_w_ukv(w_ukv[l])
        q, k, v, z, xbc, dt = _proj_call(h, pos_b, invf, row(pre_mix_norm[l]), _pack_w_in(w_in[l]),
                                         row(q_norm[l]), _pack_w_uq(w_uq[l]), row(kv_norm[l]), w_uk, w_uv)
        y_att = _attn_call(q, k, v, batch, seq)
        y_ssd = _ssd_call(z, xbc, dt, conv_w[l].astype(F32), row(conv_b[l]), _per_lane(dt_bias[l]),
                          _per_lane(a_log[l]), _per_lane(d_skip[l]), row(ssd_norm[l]), batch, seq)
        h = _mlp_call(h, y_att, y_ssd, w_out[l].astype(BF16), row(post_mix_norm[l]), row(pre_mlp_norm[l]),
                      w_up[l].astype(BF16), w_down[l].astype(BF16), row(post_mlp_norm[l]))
    return h.reshape(batch, seq, D_MODEL)
```

```python
import functools
import math

import jax
import jax.numpy as jnp
from jax import lax
from jax.experimental import pallas as pl
from jax.experimental.pallas import tpu as pltpu

F32 = jnp.float32
BF16 = jnp.bfloat16

D_MODEL = 1024
CHUNK = 64
EPS = 1e-6
MLA_HEADS = 8
QK_NOPE_DIM = 64
QK_ROPE_DIM = 32
V_HEAD_DIM = 64
Q_LORA_RANK = 768
KV_LORA_RANK = 256
ROPE_THETA = 10000.0
MLA_WIDTH = MLA_HEADS * V_HEAD_DIM
SSD_HEADS = 8
SSD_HEAD_DIM = 64
SSD_INNER = SSD_HEADS * SSD_HEAD_DIM
SSD_GROUPS = 2
SSD_STATE = 128
CONV_WIDTH = 4
CONV_DIM = SSD_INNER + 2 * SSD_GROUPS * SSD_STATE
D_FF = 4 * D_MODEL

LANES = 128
SUBLANES = 8
HEAD_PAD = LANES
HALF_ROPE = QK_ROPE_DIM // 2
ROPE_LO = QK_NOPE_DIM
ROPE_MID = QK_NOPE_DIM + HALF_ROPE
ROPE_HI = QK_NOPE_DIM + QK_ROPE_DIM
HEADS_PER_GROUP = SSD_HEADS // SSD_GROUPS
GROUP_WIDTH = HEADS_PER_GROUP * SSD_HEAD_DIM

C_Q0, C_Q1 = 0, Q_LORA_RANK
C_KV0, C_KV1 = C_Q1, C_Q1 + KV_LORA_RANK
C_KR0, C_KR1 = C_KV1, C_KV1 + HEAD_PAD
C_Z0, C_Z1 = C_KR1, C_KR1 + SSD_INNER
C_X0, C_X1 = C_Z1, C_Z1 + CONV_DIM
C_DT0, C_DT1 = C_X1, C_X1 + SSD_INNER
PROJ_COLS = C_DT1

BF16_SUBLANES = 2 * SUBLANES
VT_ROWS = V_HEAD_DIM + BF16_SUBLANES

ATTN_ROWS = 512
PROJ_ROWS = ATTN_ROWS
SSD_ROWS = 512
MLP_ROWS = 512
FF_BLOCK = 1024
VMEM_LIMIT = 56 * 1024 * 1024


def _rms(x, w):
    return x * lax.rsqrt(jnp.mean(x * x, axis=-1, keepdims=True) + EPS) * w


def _const_spec(shape):
    zeros = (0,) * len(shape)
    return pl.BlockSpec(shape, lambda *_: zeros, pipeline_mode=pl.Buffered(1))


def _proj_body(h_ref, pos_ref, invf_ref, prew_ref, w1_ref, qnw_ref, wuq_ref, kvnw_ref, wuk_ref, wuv_ref,
               q_ref, k_ref, vt_ref, z_ref, xbc_ref, dt_ref):
    u = _rms(h_ref[...], prew_ref[...]).astype(BF16)

    def seg(c0, c1):
        return jnp.dot(u, w1_ref[:, c0:c1], preferred_element_type=F32)

    c_q = seg(C_Q0, C_Q1)
    c_kv = seg(C_KV0, C_KV1)
    k_rope = seg(C_KR0, C_KR1)
    z_ref[...] = seg(C_Z0, C_Z1)
    xbc_ref[...] = seg(C_X0, C_X1)
    dt_ref[...] = seg(C_DT0, C_DT1)

    lane = lax.broadcasted_iota(jnp.int32, (1, LANES), 1)
    first_half = (lane >= ROPE_LO) & (lane < ROPE_MID)
    second_half = (lane >= ROPE_MID) & (lane < ROPE_HI)
    ang = pos_ref[...].astype(F32) * invf_ref[...]
    cos = jnp.where(first_half | second_half, jnp.cos(ang), 1.0)
    sin = jnp.sin(ang)
    sin_first = jnp.where(first_half, -sin, 0.0)
    sin_second = jnp.where(second_half, sin, 0.0)

    def rope(t):
        return (t * cos + pltpu.roll(t, LANES - HALF_ROPE, 1) * sin_first
                + pltpu.roll(t, HALF_ROPE, 1) * sin_second)

    scale = (QK_NOPE_DIM + QK_ROPE_DIM) ** -0.5 * math.log2(math.e)
    qf = jnp.dot(_rms(c_q, qnw_ref[...]).astype(BF16), wuq_ref[...], preferred_element_type=F32)
    c_kv_n = _rms(c_kv, kvnw_ref[...]).astype(BF16)
    kf = jnp.dot(c_kv_n, wuk_ref[...], preferred_element_type=F32)
    k_rope = rope(k_rope)
    for h in range(MLA_HEADS):
        blk = slice(h * HEAD_PAD, (h + 1) * HEAD_PAD)
        q_ref[h] = (rope(qf[:, blk]) * scale).astype(BF16)
        k_ref[h] = (kf[:, blk] + k_rope).astype(BF16)
    v_t = jnp.dot(c_kv_n, wuv_ref[...], preferred_element_type=F32).T
    ones = jnp.ones((BF16_SUBLANES, v_t.shape[1]), BF16)
    for h in range(MLA_HEADS):
        vt_ref[h, 0, 0:V_HEAD_DIM, :] = v_t[h * V_HEAD_DIM:(h + 1) * V_HEAD_DIM, :].astype(BF16)
        vt_ref[h, 0, V_HEAD_DIM:VT_ROWS, :] = ones


def _proj_call(h, pos_b, invf, prew, w1, qnw, wuq, kvnw, wuk, wuv):
    T = h.shape[0]
    rows = PROJ_ROWS
    row_spec = lambda cols: pl.BlockSpec((rows, cols), lambda i: (i, 0))
    head_spec = pl.BlockSpec((MLA_HEADS, rows, HEAD_PAD), lambda i: (0, i, 0))
    return pl.pallas_call(
        _proj_body,
        grid=(T // rows,),
        in_specs=[row_spec(D_MODEL), row_spec(LANES), _const_spec((1, LANES)), _const_spec((1, D_MODEL)),
                  _const_spec((D_MODEL, PROJ_COLS)), _const_spec((1, Q_LORA_RANK)),
                  _const_spec((Q_LORA_RANK, MLA_HEADS * HEAD_PAD)), _const_spec((1, KV_LORA_RANK)),
                  _const_spec((KV_LORA_RANK, MLA_HEADS * HEAD_PAD)), _const_spec((KV_LORA_RANK, MLA_WIDTH))],
        out_specs=[head_spec, head_spec,
                   pl.BlockSpec((MLA_HEADS, 1, VT_ROWS, rows), lambda i: (0, i, 0, 0)),
                   row_spec(SSD_INNER), row_spec(CONV_DIM), row_spec(SSD_INNER)],
        out_shape=[jax.ShapeDtypeStruct((MLA_HEADS, T, HEAD_PAD), BF16),
                   jax.ShapeDtypeStruct((MLA_HEADS, T, HEAD_PAD), BF16),
                   jax.ShapeDtypeStruct((MLA_HEADS, T // rows, VT_ROWS, rows), BF16),
                   jax.ShapeDtypeStruct((T, SSD_INNER), F32),
                   jax.ShapeDtypeStruct((T, CONV_DIM), F32),
                   jax.ShapeDtypeStruct((T, SSD_INNER), F32)],
        compiler_params=pltpu.CompilerParams(dimension_semantics=("parallel",), vmem_limit_bytes=VMEM_LIMIT),
        name="proj",
    )(h, pos_b, invf, prew, w1, qnw, wuq, kvnw, wuk, wuv)


def _attn_body(q_ref, k_ref, vt_ref, o_ref, s0_scr, s1_scr, m_scr, acc_scr, *, rows):
    qi = pl.program_id(2)
    s_scr = (s0_scr, s1_scr)

    def scores(j, slot):
        start = pl.multiple_of(j * rows, rows)
        for hh in range(2):
            k = k_ref[hh, pl.ds(start, rows), :]
            s_scr[slot][hh] = lax.dot_general(k, q_ref[hh], (((1,), (1,)), ((), ())),
                                              preferred_element_type=F32)

    def update(j, slot, diagonal=False):
        for hh in range(2):
            s_t = s_scr[slot][hh]
            if diagonal:
                k_chunk = lax.broadcasted_iota(jnp.int32, (rows, rows), 0) // CHUNK
                q_chunk = lax.broadcasted_iota(jnp.int32, (rows, rows), 1) // CHUNK
                s_t = jnp.where(k_chunk <= q_chunk, s_t, -jnp.inf)
            m_prev = m_scr[hh]
            m_new = jnp.maximum(m_prev, jnp.max(s_t, axis=0, keepdims=True))
            p_t = jnp.exp2(s_t - m_new).astype(BF16)
            contrib = jnp.dot(vt_ref[hh, j], p_t, preferred_element_type=F32)
            acc_scr[hh] = acc_scr[hh] * jnp.exp2(m_prev - m_new) + contrib
            m_scr[hh] = m_new

    m_scr[...] = jnp.full_like(m_scr, -jnp.inf)
    acc_scr[...] = jnp.zeros_like(acc_scr)
    scores(0, 0)

    def pair(i, carry):
        scores(2 * i + 1, 1)
        update(2 * i, 0)
        scores(2 * i + 2, 0)
        update(2 * i + 1, 1)
        return carry

    lax.fori_loop(0, qi // 2, pair, 0)

    @pl.when(qi % 2 == 0)
    def _():
        update(qi, 0, diagonal=True)

    @pl.when(qi % 2 == 1)
    def _():
        scores(qi, 1)
        update(qi - 1, 0)
        update(qi, 1, diagonal=True)

    out_t = [acc_scr[hh, 0:V_HEAD_DIM, :] / acc_scr[hh, V_HEAD_DIM:V_HEAD_DIM + 1, :] for hh in range(2)]
    o_ref[...] = jnp.concatenate(out_t, axis=0).T.astype(o_ref.dtype)


def _attn_call(q, k, vt, batch, seq):
    rows = ATTN_ROWS
    n_q = seq // rows
    pairs = MLA_HEADS // 2
    return pl.pallas_call(
        functools.partial(_attn_body, rows=rows),
        grid=(batch, pairs, n_q),
        in_specs=[pl.BlockSpec((2, rows, HEAD_PAD), lambda b, p, i: (p, b * n_q + i, 0)),
                  pl.BlockSpec((2, seq, HEAD_PAD), lambda b, p, i: (p, b, 0)),
                  pl.BlockSpec((2, n_q, VT_ROWS, rows), lambda b, p, i: (p, b, 0, 0))],
        out_specs=pl.BlockSpec((rows, 2 * V_HEAD_DIM), lambda b, p, i: (b * n_q + i, p)),
        out_shape=jax.ShapeDtypeStruct((batch * seq, MLA_WIDTH), BF16),
        scratch_shapes=[pltpu.VMEM((2, rows, rows), F32), pltpu.VMEM((2, rows, rows), F32),
                        pltpu.VMEM((2, 1, rows), F32),
                        pltpu.VMEM((2, VT_ROWS, rows), F32)],
        compiler_params=pltpu.CompilerParams(dimension_semantics=("parallel", "parallel", "arbitrary"),
                                             vmem_limit_bytes=VMEM_LIMIT),
        name="attn",
    )(q, k, vt)


def _ssd_body(z_ref, xbc_ref, dt_ref, convw_ref, convb_ref, dtb_ref, alog_ref, dskip_ref, normw_ref,
              o_ref, ext_scr, act_scr, dts_scr, y_scr, state_scr, *, rows):
    tail = SUBLANES
    L = CHUNK

    @pl.when(pl.program_id(1) == 0)
    def _():
        ext_scr[0:tail, :] = jnp.zeros((tail, CONV_DIM), F32)
        state_scr[...] = jnp.zeros_like(state_scr)

    ext_scr[tail:tail + rows, :] = xbc_ref[...]
    conv = ext_scr[tail:tail + rows, :] * convw_ref[CONV_WIDTH - 1:CONV_WIDTH, :]
    for back in range(1, CONV_WIDTH):
        w_row = convw_ref[CONV_WIDTH - 1 - back:CONV_WIDTH - back, :]
        conv = conv + ext_scr[tail - back:tail - back + rows, :] * w_row
    conv = conv + convb_ref[...]
    act_scr[...] = conv * jax.nn.sigmoid(conv)
    ext_scr[0:tail, :] = ext_scr[rows:rows + tail, :]
    dts_scr[...] = jax.nn.softplus(dt_ref[...] + dtb_ref[...])

    a_coef = -jnp.exp(alog_ref[...])
    row_i = lax.broadcasted_iota(jnp.int32, (L, SSD_INNER), 0)
    col_j = lax.broadcasted_iota(jnp.int32, (L, SSD_INNER), 1) % SSD_HEAD_DIM
    upper = (row_i <= col_j).astype(F32)
    lower = col_j <= row_i
    tril = (lax.broadcasted_iota(jnp.int32, (L, L), 1) <= lax.broadcasted_iota(jnp.int32, (L, L), 0)).astype(F32)
    blk_r = lax.broadcasted_iota(jnp.int32, (GROUP_WIDTH, GROUP_WIDTH), 0) // SSD_HEAD_DIM
    blk_c = lax.broadcasted_iota(jnp.int32, (GROUP_WIDTH, GROUP_WIDTH), 1) // SSD_HEAD_DIM
    same_head = blk_r == blk_c

    def chunk(c, carry):
        r0 = pl.multiple_of(c * L, L)
        xs = act_scr[pl.ds(r0, L), 0:SSD_INNER]
        dt = dts_scr[pl.ds(r0, L), :]
        a = dt * a_coef
        xdt = xs * dt
        a_cs = jnp.dot(tril, a, preferred_element_type=F32, precision=lax.Precision.HIGHEST)
        a_cs_row = jnp.sum(a * upper, axis=0, keepdims=True)
        decay_in = jnp.exp(jnp.where(lower, a_cs - a_cs_row, -jnp.inf))
        a_last = a_cs[L - 1:L, :]
        decay_to_end = jnp.exp(a_last - a_cs)
        decay_from_start = jnp.exp(a_cs)
        chunk_decay = jnp.exp(a_last)
        for g in range(SSD_GROUPS):
            gl = slice(g * GROUP_WIDTH, (g + 1) * GROUP_WIDTH)
            b_off = SSD_INNER + g * SSD_STATE
            c_off = SSD_INNER + SSD_GROUPS * SSD_STATE + g * SSD_STATE
            b_f32 = act_scr[pl.ds(r0, L), b_off:b_off + SSD_STATE]
            b_g = b_f32.astype(BF16)
            c_g = act_scr[pl.ds(r0, L), c_off:c_off + SSD_STATE].astype(BF16)
            b_rep = jnp.concatenate([b_g] * HEADS_PER_GROUP, axis=0)
            cb = lax.dot_general(c_g, b_rep, (((1,), (1,)), ((), ())), preferred_element_type=F32)
            scores = (cb * decay_in[:, gl]).astype(BF16)
            x_g = xdt[:, gl]
            x_rep = jnp.concatenate([x_g] * HEADS_PER_GROUP, axis=0)
            x_diag = jnp.where(same_head, x_rep, 0.0).astype(BF16)
            y_diag = jnp.dot(scores, x_diag, preferred_element_type=F32)
            state = state_scr[g]
            y_off = jnp.dot(c_g, state.astype(BF16), preferred_element_type=F32) * decay_from_start[:, gl]
            x_dec = (x_g * decay_to_end[:, gl]).astype(BF16)
            new_state = jnp.dot(b_f32.T.astype(BF16), x_dec, preferred_element_type=F32)
            state_scr[g] = state * chunk_decay[:, gl] + new_state
            y_scr[pl.ds(r0, L), gl] = y_diag + y_off + dskip_ref[:, gl] * xs[:, gl]
        return carry

    lax.fori_loop(0, rows // L, chunk, 0)

    zf = z_ref[...]
    y = y_scr[...] * (zf * jax.nn.sigmoid(zf))
    for g in range(SSD_GROUPS):
        gl = slice(g * GROUP_WIDTH, (g + 1) * GROUP_WIDTH)
        yg = y[:, gl]
        yg = yg * lax.rsqrt(jnp.mean(yg * yg, axis=-1, keepdims=True) + EPS)
        o_ref[:, gl] = (yg * normw_ref[:, gl]).astype(o_ref.dtype)


def _ssd_call(z, xbc, dt, convw, convb, dtb, alog, dskip, normw, batch, seq):
    rows = SSD_ROWS
    n_t = seq // rows
    row_spec = lambda cols: pl.BlockSpec((rows, cols), lambda b, i: (b * n_t + i, 0))
    return pl.pallas_call(
        functools.partial(_ssd_body, rows=rows),
        grid=(batch, n_t),
        in_specs=[row_spec(SSD_INNER), row_spec(CONV_DIM), row_spec(SSD_INNER),
                  _const_spec((CONV_WIDTH, CONV_DIM)), _const_spec((1, CONV_DIM)), _const_spec((1, SSD_INNER)),
                  _const_spec((1, SSD_INNER)), _const_spec((1, SSD_INNER)), _const_spec((1, SSD_INNER))],
        out_specs=row_spec(SSD_INNER),
        out_shape=jax.ShapeDtypeStruct((batch * seq, SSD_INNER), BF16),
        scratch_shapes=[pltpu.VMEM((rows + SUBLANES, CONV_DIM), F32), pltpu.VMEM((rows, CONV_DIM), F32),
                        pltpu.VMEM((rows, SSD_INNER), F32), pltpu.VMEM((rows, SSD_INNER), F32),
                        pltpu.VMEM((SSD_GROUPS, SSD_STATE, GROUP_WIDTH), F32)],
        compiler_params=pltpu.CompilerParams(dimension_semantics=("arbitrary", "arbitrary"),
                                             vmem_limit_bytes=VMEM_LIMIT),
        name="ssd",
    )(z, xbc, dt, convw, convb, dtb, alog, dskip, normw)


def _mlp_body(h_ref, ya_ref, ys_ref, wout_ref, postmix_ref, premlp_ref, wup_ref, wdn_ref, postmlp_ref, o_ref):
    mixed = (jnp.dot(ya_ref[...], wout_ref[0:MLA_WIDTH, :], preferred_element_type=F32)
             + jnp.dot(ys_ref[...], wout_ref[MLA_WIDTH:, :], preferred_element_type=F32))
    h1 = h_ref[...] + _rms(mixed, postmix_ref[...])
    m = _rms(h1, premlp_ref[...]).astype(BF16)
    acc = None
    for c in range(D_FF // FF_BLOCK):
        ff = slice(c * FF_BLOCK, (c + 1) * FF_BLOCK)
        up = jnp.dot(m, wup_ref[:, ff], preferred_element_type=F32)
        act = jnp.square(jnp.maximum(up, 0.0)).astype(BF16)
        part = jnp.dot(act, wdn_ref[ff, :], preferred_element_type=F32)
        acc = part if acc is None else acc + part
    o_ref[...] = h1 + _rms(acc, postmlp_ref[...])


def _mlp_call(h, ya, ys, wout, postmix, premlp, wup, wdn, postmlp):
    T = h.shape[0]
    rows = MLP_ROWS
    row_spec = lambda cols: pl.BlockSpec((rows, cols), lambda i: (i, 0))
    return pl.pallas_call(
        _mlp_body,
        grid=(T // rows,),
        in_specs=[row_spec(D_MODEL), row_spec(MLA_WIDTH), row_spec(SSD_INNER),
                  _const_spec((MLA_WIDTH + SSD_INNER, D_MODEL)), _const_spec((1, D_MODEL)),
                  _const_spec((1, D_MODEL)), _const_spec((D_MODEL, D_FF)), _const_spec((D_FF, D_MODEL)),
                  _const_spec((1, D_MODEL))],
        out_specs=row_spec(D_MODEL),
        out_shape=jax.ShapeDtypeStruct((T, D_MODEL), F32),
        compiler_params=pltpu.CompilerParams(dimension_semantics=("parallel",), vmem_limit_bytes=VMEM_LIMIT),
        name="mlp",
    )(h, ya, ys, wout, postmix, premlp, wup, wdn, postmlp)


def _pack_w_in(w_in):
    s1 = Q_LORA_RANK
    s2 = s1 + KV_LORA_RANK
    s3 = s2 + QK_ROPE_DIM
    s4 = s3 + SSD_INNER
    s5 = s4 + CONV_DIM
    zeros = lambda n: jnp.zeros((D_MODEL, n), w_in.dtype)
    k_rope = jnp.concatenate([zeros(ROPE_LO), w_in[:, s2:s3], zeros(HEAD_PAD - ROPE_HI)], axis=1)
    dt_rep = jnp.repeat(w_in[:, s5:], SSD_HEAD_DIM, axis=1)
    return jnp.concatenate([w_in[:, :s2], k_rope, w_in[:, s3:s5], dt_rep], axis=1).astype(BF16)


def _pack_w_uq(w_uq):
    w = w_uq.reshape(Q_LORA_RANK, MLA_HEADS, QK_NOPE_DIM + QK_ROPE_DIM)
    pad = jnp.zeros((Q_LORA_RANK, MLA_HEADS, HEAD_PAD - ROPE_HI), w_uq.dtype)
    return jnp.concatenate([w, pad], axis=-1).reshape(Q_LORA_RANK, MLA_HEADS * HEAD_PAD).astype(BF16)


def _pack_w_ukv(w_ukv):
    w = w_ukv.reshape(KV_LORA_RANK, MLA_HEADS, QK_NOPE_DIM + V_HEAD_DIM)
    pad = jnp.zeros((KV_LORA_RANK, MLA_HEADS, HEAD_PAD - QK_NOPE_DIM), w_ukv.dtype)
    w_uk = jnp.concatenate([w[..., :QK_NOPE_DIM], pad], axis=-1).reshape(KV_LORA_RANK, MLA_HEADS * HEAD_PAD)
    w_uv = w[..., QK_NOPE_DIM:].reshape(KV_LORA_RANK, MLA_WIDTH)
    return w_uk.astype(BF16), w_uv.astype(BF16)


def _per_lane(v):
    return jnp.repeat(v.astype(F32), SSD_HEAD_DIM)[None, :]


def kernel(x, positions, pre_mix_norm, w_in, q_norm, w_uq, kv_norm, w_ukv, conv_w, conv_b, dt_bias, a_log,
           d_skip, ssd_norm, w_out, post_mix_norm, pre_mlp_norm, w_up, w_down, post_mlp_norm):
    batch, seq, _ = x.shape
    depth = w_in.shape[0]
    T = batch * seq
    assert seq % ATTN_ROWS == 0 and seq % SSD_ROWS == 0 and T % PROJ_ROWS == 0 and T % MLP_ROWS == 0

    inv_freq = ROPE_THETA ** (-jnp.arange(0, QK_ROPE_DIM, 2, dtype=F32) / QK_ROPE_DIM)
    invf = jnp.concatenate([jnp.zeros((ROPE_LO,), F32), inv_freq, inv_freq,
                            jnp.zeros((HEAD_PAD - ROPE_HI,), F32)])[None, :]
    pos_b = jnp.broadcast_to(positions.reshape(T, 1), (T, LANES))
    row = lambda v: v.astype(F32)[None, :]

    h = x.reshape(T, D_MODEL)
    for l in range(depth):
        w_uk, w_uv = _pack_w_ukv(w_ukv[l])
        q, k, vt, z, xbc, dt = _proj_call(h, pos_b, invf, row(pre_mix_norm[l]), _pack_w_in(w_in[l]),
                                         row(q_norm[l]), _pack_w_uq(w_uq[l]), row(kv_norm[l]), w_uk, w_uv)
        y_att = _attn_call(q, k, vt, batch, seq)
        y_ssd = _ssd_call(z, xbc, dt, conv_w[l].astype(F32), row(conv_b[l]), _per_lane(dt_bias[l]),
                          _per_lane(a_log[l]), _per_lane(d_skip[l]), row(ssd_norm[l]), batch, seq)
        h = _mlp_call(h, y_att, y_ssd, w_out[l].astype(BF16), row(post_mix_norm[l]), row(pre_mlp_norm[l]),
                      w_up[l].astype(BF16), w_down[l].astype(BF16), row(post_mlp_norm[l]))
    return h.reshape(batch, seq, D_MODEL)
```

```python
import functools
import math

import jax
import jax.numpy as jnp
from jax import lax
from jax.experimental import pallas as pl
from jax.experimental.pallas import tpu as pltpu

F32 = jnp.float32
BF16 = jnp.bfloat16

D_MODEL = 1024
CHUNK = 64
EPS = 1e-6
MLA_HEADS = 8
QK_NOPE_DIM = 64
QK_ROPE_DIM = 32
V_HEAD_DIM = 64
Q_LORA_RANK = 768
KV_LORA_RANK = 256
ROPE_THETA = 10000.0
MLA_WIDTH = MLA_HEADS * V_HEAD_DIM
SSD_HEADS = 8
SSD_HEAD_DIM = 64
SSD_INNER = SSD_HEADS * SSD_HEAD_DIM
SSD_GROUPS = 2
SSD_STATE = 128
CONV_WIDTH = 4
CONV_DIM = SSD_INNER + 2 * SSD_GROUPS * SSD_STATE
D_FF = 4 * D_MODEL

LANES = 128
SUBLANES = 8
HEAD_PAD = LANES
HALF_ROPE = QK_ROPE_DIM // 2
ROPE_LO = QK_NOPE_DIM
ROPE_MID = QK_NOPE_DIM + HALF_ROPE
ROPE_HI = QK_NOPE_DIM + QK_ROPE_DIM
HEADS_PER_GROUP = SSD_HEADS // SSD_GROUPS
GROUP_WIDTH = HEADS_PER_GROUP * SSD_HEAD_DIM

C_Q0, C_Q1 = 0, Q_LORA_RANK
C_KV0, C_KV1 = C_Q1, C_Q1 + KV_LORA_RANK
C_KR0, C_KR1 = C_KV1, C_KV1 + HEAD_PAD
C_Z0, C_Z1 = C_KR1, C_KR1 + SSD_INNER
C_X0, C_X1 = C_Z1, C_Z1 + CONV_DIM
C_DT0, C_DT1 = C_X1, C_X1 + SSD_INNER
PROJ_COLS = C_DT1

BF16_SUBLANES = 2 * SUBLANES
VT_ROWS = V_HEAD_DIM + BF16_SUBLANES

ATTN_ROWS = 512
ATTN_HEADS = 4
PROJ_ROWS = ATTN_ROWS
SSD_ROWS = 512
MLP_ROWS = 512
FF_BLOCK = 1024
VMEM_LIMIT = 56 * 1024 * 1024


def _rms(x, w):
    return x * lax.rsqrt(jnp.mean(x * x, axis=-1, keepdims=True) + EPS) * w


def _const_spec(shape):
    zeros = (0,) * len(shape)
    return pl.BlockSpec(shape, lambda *_: zeros, pipeline_mode=pl.Buffered(1))


def _proj_body(h_ref, pos_ref, invf_ref, prew_ref, w1_ref, qnw_ref, wuq_ref, kvnw_ref, wuk_ref, wuv_ref,
               q_ref, k_ref, vt_ref, z_ref, xbc_ref, dt_ref):
    u = _rms(h_ref[...], prew_ref[...]).astype(BF16)

    def seg(c0, c1):
        return jnp.dot(u, w1_ref[:, c0:c1], preferred_element_type=F32)

    c_q = seg(C_Q0, C_Q1)
    c_kv = seg(C_KV0, C_KV1)
    k_rope = seg(C_KR0, C_KR1)
    z_ref[...] = seg(C_Z0, C_Z1)
    xbc_ref[...] = seg(C_X0, C_X1)
    dt_ref[...] = seg(C_DT0, C_DT1)

    lane = lax.broadcasted_iota(jnp.int32, (1, LANES), 1)
    first_half = (lane >= ROPE_LO) & (lane < ROPE_MID)
    second_half = (lane >= ROPE_MID) & (lane < ROPE_HI)
    ang = pos_ref[...].astype(F32) * invf_ref[...]
    cos = jnp.where(first_half | second_half, jnp.cos(ang), 1.0)
    sin = jnp.sin(ang)
    sin_first = jnp.where(first_half, -sin, 0.0)
    sin_second = jnp.where(second_half, sin, 0.0)

    def rope(t):
        return (t * cos + pltpu.roll(t, LANES - HALF_ROPE, 1) * sin_first
                + pltpu.roll(t, HALF_ROPE, 1) * sin_second)

    scale = (QK_NOPE_DIM + QK_ROPE_DIM) ** -0.5 * math.log2(math.e)
    qf = jnp.dot(_rms(c_q, qnw_ref[...]).astype(BF16), wuq_ref[...], preferred_element_type=F32)
    c_kv_n = _rms(c_kv, kvnw_ref[...]).astype(BF16)
    kf = jnp.dot(c_kv_n, wuk_ref[...], preferred_element_type=F32)
    k_rope = rope(k_rope)
    for h in range(MLA_HEADS):
        blk = slice(h * HEAD_PAD, (h + 1) * HEAD_PAD)
        q_ref[h] = (rope(qf[:, blk]) * scale).astype(BF16)
        k_ref[h] = (kf[:, blk] + k_rope).astype(BF16)
    v_t = jnp.dot(c_kv_n, wuv_ref[...], preferred_element_type=F32).T
    ones = jnp.ones((BF16_SUBLANES, v_t.shape[1]), BF16)
    for h in range(MLA_HEADS):
        vt_ref[h, 0, 0:V_HEAD_DIM, :] = v_t[h * V_HEAD_DIM:(h + 1) * V_HEAD_DIM, :].astype(BF16)
        vt_ref[h, 0, V_HEAD_DIM:VT_ROWS, :] = ones


def _proj_call(h, pos_b, invf, prew, w1, qnw, wuq, kvnw, wuk, wuv):
    T = h.shape[0]
    rows = PROJ_ROWS
    row_spec = lambda cols: pl.BlockSpec((rows, cols), lambda i: (i, 0))
    head_spec = pl.BlockSpec((MLA_HEADS, rows, HEAD_PAD), lambda i: (0, i, 0))
    return pl.pallas_call(
        _proj_body,
        grid=(T // rows,),
        in_specs=[row_spec(D_MODEL), row_spec(LANES), _const_spec((1, LANES)), _const_spec((1, D_MODEL)),
                  _const_spec((D_MODEL, PROJ_COLS)), _const_spec((1, Q_LORA_RANK)),
                  _const_spec((Q_LORA_RANK, MLA_HEADS * HEAD_PAD)), _const_spec((1, KV_LORA_RANK)),
                  _const_spec((KV_LORA_RANK, MLA_HEADS * HEAD_PAD)), _const_spec((KV_LORA_RANK, MLA_WIDTH))],
        out_specs=[head_spec, head_spec,
                   pl.BlockSpec((MLA_HEADS, 1, VT_ROWS, rows), lambda i: (0, i, 0, 0)),
                   row_spec(SSD_INNER), row_spec(CONV_DIM), row_spec(SSD_INNER)],
        out_shape=[jax.ShapeDtypeStruct((MLA_HEADS, T, HEAD_PAD), BF16),
                   jax.ShapeDtypeStruct((MLA_HEADS, T, HEAD_PAD), BF16),
                   jax.ShapeDtypeStruct((MLA_HEADS, T // rows, VT_ROWS, rows), BF16),
                   jax.ShapeDtypeStruct((T, SSD_INNER), F32),
                   jax.ShapeDtypeStruct((T, CONV_DIM), F32),
                   jax.ShapeDtypeStruct((T, SSD_INNER), F32)],
        compiler_params=pltpu.CompilerParams(dimension_semantics=("parallel",), vmem_limit_bytes=VMEM_LIMIT),
        name="proj",
    )(h, pos_b, invf, prew, w1, qnw, wuq, kvnw, wuk, wuv)


def _attn_body(q_ref, k_ref, vt_ref, bias_ref, o_ref, s0_scr, s1_scr, m_scr, acc_scr, *, rows):
    qi = pl.program_id(2)
    s_scr = (s0_scr, s1_scr)

    def scores(hh, j, slot):
        start = pl.multiple_of(j * rows, rows)
        k = k_ref[hh, pl.ds(start, rows), :]
        s_scr[slot][hh] = lax.dot_general(k, q_ref[hh], (((1,), (1,)), ((), ())), preferred_element_type=F32)

    def update(hh, j, slot, diagonal=False):
        s_t = s_scr[slot][hh]
        if diagonal:
            s_t = s_t + bias_ref[...]
        m_prev = m_scr[hh]
        m_new = jnp.maximum(m_prev, jnp.max(s_t, axis=0, keepdims=True))
        p_t = jnp.exp2(s_t - m_new).astype(BF16)
        contrib = jnp.dot(vt_ref[hh, j], p_t, preferred_element_type=F32)
        acc_scr[hh] = acc_scr[hh] * jnp.exp2(m_prev - m_new) + contrib
        m_scr[hh] = m_new

    m_scr[...] = jnp.full_like(m_scr, -jnp.inf)
    acc_scr[...] = jnp.zeros_like(acc_scr)
    for hh in range(ATTN_HEADS):
        scores(hh, 0, 0)

    def full_tile(j, carry):
        for slot in range(2):
            @pl.when(j % 2 == slot)
            def _():
                for hh in range(ATTN_HEADS):
                    scores(hh, j + 1, 1 - slot)
                    update(hh, j, slot)
        return carry

    lax.fori_loop(0, qi, full_tile, 0)
    for slot in range(2):
        @pl.when(qi % 2 == slot)
        def _():
            for hh in range(ATTN_HEADS):
                update(hh, qi, slot, diagonal=True)

    out_t = [acc_scr[hh, 0:V_HEAD_DIM, :] / acc_scr[hh, V_HEAD_DIM:V_HEAD_DIM + 1, :]
             for hh in range(ATTN_HEADS)]
    o_ref[...] = jnp.concatenate(out_t, axis=0).T.astype(o_ref.dtype)


def _attn_call(q, k, vt, batch, seq):
    rows = ATTN_ROWS
    n_q = seq // rows
    groups = MLA_HEADS // ATTN_HEADS
    chunk_of = jnp.arange(rows, dtype=jnp.int32) // CHUNK
    bias = jnp.where(chunk_of[:, None] <= chunk_of[None, :], 0.0, -jnp.inf).astype(F32)
    return pl.pallas_call(
        functools.partial(_attn_body, rows=rows),
        grid=(batch, groups, n_q),
        in_specs=[pl.BlockSpec((ATTN_HEADS, rows, HEAD_PAD), lambda b, g, i: (g, b * n_q + i, 0)),
                  pl.BlockSpec((ATTN_HEADS, seq, HEAD_PAD), lambda b, g, i: (g, b, 0)),
                  pl.BlockSpec((ATTN_HEADS, n_q, VT_ROWS, rows), lambda b, g, i: (g, b, 0, 0)),
                  _const_spec((rows, rows))],
        out_specs=pl.BlockSpec((rows, ATTN_HEADS * V_HEAD_DIM), lambda b, g, i: (b * n_q + i, g)),
        out_shape=jax.ShapeDtypeStruct((batch * seq, MLA_WIDTH), BF16),
        scratch_shapes=[pltpu.VMEM((ATTN_HEADS, rows, rows), F32), pltpu.VMEM((ATTN_HEADS, rows, rows), F32),
                        pltpu.VMEM((ATTN_HEADS, 1, rows), F32),
                        pltpu.VMEM((ATTN_HEADS, VT_ROWS, rows), F32)],
        compiler_params=pltpu.CompilerParams(dimension_semantics=("parallel", "parallel", "arbitrary"),
                                             vmem_limit_bytes=VMEM_LIMIT),
        name="attn",
    )(q, k, vt, bias)


def _ssd_body(z_ref, xbc_ref, dt_ref, convw_ref, convb_ref, dtb_ref, alog_ref, dskip_ref, normw_ref,
              o_ref, ext_scr, act_scr, dts_scr, y_scr, state_scr, *, rows):
    tail = SUBLANES
    L = CHUNK

    @pl.when(pl.program_id(1) == 0)
    def _():
        ext_scr[0:tail, :] = jnp.zeros((tail, CONV_DIM), F32)
        state_scr[...] = jnp.zeros_like(state_scr)

    ext_scr[tail:tail + rows, :] = xbc_ref[...]
    conv = ext_scr[tail:tail + rows, :] * convw_ref[CONV_WIDTH - 1:CONV_WIDTH, :]
    for back in range(1, CONV_WIDTH):
        w_row = convw_ref[CONV_WIDTH - 1 - back:CONV_WIDTH - back, :]
        conv = conv + ext_scr[tail - back:tail - back + rows, :] * w_row
    conv = conv + convb_ref[...]
    act_scr[...] = conv * jax.nn.sigmoid(conv)
    ext_scr[0:tail, :] = ext_scr[rows:rows + tail, :]
    dts_scr[...] = jax.nn.softplus(dt_ref[...] + dtb_ref[...])

    a_coef = -jnp.exp(alog_ref[...])
    row_i = lax.broadcasted_iota(jnp.int32, (L, SSD_INNER), 0)
    col_j = lax.broadcasted_iota(jnp.int32, (L, SSD_INNER), 1) % SSD_HEAD_DIM
    upper = (row_i <= col_j).astype(F32)
    lower = col_j <= row_i
    tril = (lax.broadcasted_iota(jnp.int32, (L, L), 1) <= lax.broadcasted_iota(jnp.int32, (L, L), 0)).astype(F32)
    blk_r = lax.broadcasted_iota(jnp.int32, (GROUP_WIDTH, GROUP_WIDTH), 0) // SSD_HEAD_DIM
    blk_c = lax.broadcasted_iota(jnp.int32, (GROUP_WIDTH, GROUP_WIDTH), 1) // SSD_HEAD_DIM
    same_head = blk_r == blk_c

    def chunk(c, carry):
        r0 = c * L
        xs = act_scr[pl.ds(r0, L), 0:SSD_INNER]
        dt = dts_scr[pl.ds(r0, L), :]
        a = dt * a_coef
        xdt = xs * dt
        a_cs = jnp.dot(tril, a, preferred_element_type=F32, precision=lax.Precision.HIGHEST)
        a_cs_row = jnp.sum(a * upper, axis=0, keepdims=True)
        decay_in = jnp.exp(jnp.where(lower, a_cs - a_cs_row, -jnp.inf))
        a_last = a_cs[L - 1:L, :]
        decay_to_end = jnp.exp(a_last - a_cs)
        decay_from_start = jnp.exp(a_cs)
        chunk_decay = jnp.exp(a_last)
        for g in range(SSD_GROUPS):
            gl = slice(g * GROUP_WIDTH, (g + 1) * GROUP_WIDTH)
            b_off = SSD_INNER + g * SSD_STATE
            c_off = SSD_INNER + SSD_GROUPS * SSD_STATE + g * SSD_STATE
            b_f32 = act_scr[pl.ds(r0, L), b_off:b_off + SSD_STATE]
            b_g = b_f32.astype(BF16)
            c_g = act_scr[pl.ds(r0, L), c_off:c_off + SSD_STATE].astype(BF16)
            b_rep = jnp.concatenate([b_g] * HEADS_PER_GROUP, axis=0)
            cb = lax.dot_general(c_g, b_rep, (((1,), (1,)), ((), ())), preferred_element_type=F32)
            scores = (cb * decay_in[:, gl]).astype(BF16)
            x_g = xdt[:, gl]
            x_rep = jnp.concatenate([x_g] * HEADS_PER_GROUP, axis=0)
            x_diag = jnp.where(same_head, x_rep, 0.0).astype(BF16)
            y_diag = jnp.dot(scores, x_diag, preferred_element_type=F32)
            state = state_scr[g]
            y_off = jnp.dot(c_g, state.astype(BF16), preferred_element_type=F32) * decay_from_start[:, gl]
            x_dec = (x_g * decay_to_end[:, gl]).astype(BF16)
            new_state = jnp.dot(b_f32.T.astype(BF16), x_dec, preferred_element_type=F32)
            state_scr[g] = state * chunk_decay[:, gl] + new_state
            y_scr[pl.ds(r0, L), gl] = y_diag + y_off + dskip_ref[:, gl] * xs[:, gl]
        return carry

    for c in range(rows // L):
        chunk(c, 0)

    zf = z_ref[...]
    y = y_scr[...] * (zf * jax.nn.sigmoid(zf))
    for g in range(SSD_GROUPS):
        gl = slice(g * GROUP_WIDTH, (g + 1) * GROUP_WIDTH)
        yg = y[:, gl]
        yg = yg * lax.rsqrt(jnp.mean(yg * yg, axis=-1, keepdims=True) + EPS)
        o_ref[:, gl] = (yg * normw_ref[:, gl]).astype(o_ref.dtype)


def _ssd_call(z, xbc, dt, convw, convb, dtb, alog, dskip, normw, batch, seq):
    rows = SSD_ROWS
    n_t = seq // rows
    row_spec = lambda cols: pl.BlockSpec((rows, cols), lambda b, i: (b * n_t + i, 0))
    return pl.pallas_call(
        functools.partial(_ssd_body, rows=rows),
        grid=(batch, n_t),
        in_specs=[row_spec(SSD_INNER), row_spec(CONV_DIM), row_spec(SSD_INNER),
                  _const_spec((CONV_WIDTH, CONV_DIM)), _const_spec((1, CONV_DIM)), _const_spec((1, SSD_INNER)),
                  _const_spec((1, SSD_INNER)), _const_spec((1, SSD_INNER)), _const_spec((1, SSD_INNER))],
        out_specs=row_spec(SSD_INNER),
        out_shape=jax.ShapeDtypeStruct((batch * seq, SSD_INNER), BF16),
        scratch_shapes=[pltpu.VMEM((rows + SUBLANES, CONV_DIM), F32), pltpu.VMEM((rows, CONV_DIM), F32),
                        pltpu.VMEM((rows, SSD_INNER), F32), pltpu.VMEM((rows, SSD_INNER), F32),
                        pltpu.VMEM((SSD_GROUPS, SSD_STATE, GROUP_WIDTH), F32)],
        compiler_params=pltpu.CompilerParams(dimension_semantics=("arbitrary", "arbitrary"),
                                             vmem_limit_bytes=VMEM_LIMIT),
        name="ssd",
    )(z, xbc, dt, convw, convb, dtb, alog, dskip, normw)


def _mlp_body(h_ref, ya_ref, ys_ref, wout_ref, postmix_ref, premlp_ref, wup_ref, wdn_ref, postmlp_ref, o_ref):
    mixed = (jnp.dot(ya_ref[...], wout_ref[0:MLA_WIDTH, :], preferred_element_type=F32)
             + jnp.dot(ys_ref[...], wout_ref[MLA_WIDTH:, :], preferred_element_type=F32))
    h1 = h_ref[...] + _rms(mixed, postmix_ref[...])
    m = _rms(h1, premlp_ref[...]).astype(BF16)
    acc = None
    for c in range(D_FF // FF_BLOCK):
        ff = slice(c * FF_BLOCK, (c + 1) * FF_BLOCK)
        up = jnp.dot(m, wup_ref[:, ff], preferred_element_type=F32)
        act = jnp.square(jnp.maximum(up, 0.0)).astype(BF16)
        part = jnp.dot(act, wdn_ref[ff, :], preferred_element_type=F32)
        acc = part if acc is None else acc + part
    o_ref[...] = h1 + _rms(acc, postmlp_ref[...])


def _mlp_call(h, ya, ys, wout, postmix, premlp, wup, wdn, postmlp):
    T = h.shape[0]
    rows = MLP_ROWS
    row_spec = lambda cols: pl.BlockSpec((rows, cols), lambda i: (i, 0))
    return pl.pallas_call(
        _mlp_body,
        grid=(T // rows,),
        in_specs=[row_spec(D_MODEL), row_spec(MLA_WIDTH), row_spec(SSD_INNER),
                  _const_spec((MLA_WIDTH + SSD_INNER, D_MODEL)), _const_spec((1, D_MODEL)),
                  _const_spec((1, D_MODEL)), _const_spec((D_MODEL, D_FF)), _const_spec((D_FF, D_MODEL)),
                  _const_spec((1, D_MODEL))],
        out_specs=row_spec(D_MODEL),
        out_shape=jax.ShapeDtypeStruct((T, D_MODEL), F32),
        compiler_params=pltpu.CompilerParams(dimension_semantics=("parallel",), vmem_limit_bytes=VMEM_LIMIT),
        name="mlp",
    )(h, ya, ys, wout, postmix, premlp, wup, wdn, postmlp)


def _pack_w_in(w_in):
    s1 = Q_LORA_RANK
    s2 = s1 + KV_LORA_RANK
    s3 = s2 + QK_ROPE_DIM
    s4 = s3 + SSD_INNER
    s5 = s4 + CONV_DIM
    zeros = lambda n: jnp.zeros((D_MODEL, n), w_in.dtype)
    k_rope = jnp.concatenate([zeros(ROPE_LO), w_in[:, s2:s3], zeros(HEAD_PAD - ROPE_HI)], axis=1)
    dt_rep = jnp.repeat(w_in[:, s5:], SSD_HEAD_DIM, axis=1)
    return jnp.concatenate([w_in[:, :s2], k_rope, w_in[:, s3:s5], dt_rep], axis=1).astype(BF16)


def _pack_w_uq(w_uq):
    w = w_uq.reshape(Q_LORA_RANK, MLA_HEADS, QK_NOPE_DIM + QK_ROPE_DIM)
    pad = jnp.zeros((Q_LORA_RANK, MLA_HEADS, HEAD_PAD - ROPE_HI), w_uq.dtype)
    return jnp.concatenate([w, pad], axis=-1).reshape(Q_LORA_RANK, MLA_HEADS * HEAD_PAD).astype(BF16)


def _pack_w_ukv(w_ukv):
    w = w_ukv.reshape(KV_LORA_RANK, MLA_HEADS, QK_NOPE_DIM + V_HEAD_DIM)
    pad = jnp.zeros((KV_LORA_RANK, MLA_HEADS, HEAD_PAD - QK_NOPE_DIM), w_ukv.dtype)
    w_uk = jnp.concatenate([w[..., :QK_NOPE_DIM], pad], axis=-1).reshape(KV_LORA_RANK, MLA_HEADS * HEAD_PAD)
    w_uv = w[..., QK_NOPE_DIM:].reshape(KV_LORA_RANK, MLA_WIDTH)
    return w_uk.astype(BF16), w_uv.astype(BF16)


def _per_lane(v):
    return jnp.repeat(v.astype(F32), SSD_HEAD_DIM)[None, :]


def kernel(x, positions, pre_mix_norm, w_in, q_norm, w_uq, kv_norm, w_ukv, conv_w, conv_b, dt_bias, a_log,
           d_skip, ssd_norm, w_out, post_mix_norm, pre_mlp_norm, w_up, w_down, post_mlp_norm):
    batch, seq, _ = x.shape
    depth = w_in.shape[0]
    T = batch * seq
    assert seq % ATTN_ROWS == 0 and seq % SSD_ROWS == 0 and T % PROJ_ROWS == 0 and T % MLP_ROWS == 0

    inv_freq = ROPE_THETA ** (-jnp.arange(0, QK_ROPE_DIM, 2, dtype=F32) / QK_ROPE_DIM)
    invf = jnp.concatenate([jnp.zeros((ROPE_LO,), F32), inv_freq, inv_freq,
                            jnp.zeros((HEAD_PAD - ROPE_HI,), F32)])[None, :]
    pos_b = jnp.broadcast_to(positions.reshape(T, 1), (T, LANES))
    row = lambda v: v.astype(F32)[None, :]

    h = x.reshape(T, D_MODEL)
    for l in range(depth):
        w_uk, w_uv = _pack_w_ukv(w_ukv[l])
        q, k, vt, z, xbc, dt = _proj_call(h, pos_b, invf, row(pre_mix_norm[l]), _pack_w_in(w_in[l]),
                                         row(q_norm[l]), _pack_w_uq(w_uq[l]), row(kv_norm[l]), w_uk, w_uv)
        y_att = _attn_call(q, k, vt, batch, seq)
        y_ssd = _ssd_call(z, xbc, dt, conv_w[l].astype(F32), row(conv_b[l]), _per_lane(dt_bias[l]),
                          _per_lane(a_log[l]), _per_lane(d_skip[l]), row(ssd_norm[l]), batch, seq)
        h = _mlp_call(h, y_att, y_ssd, w_out[l].astype(BF16), row(post_mix_norm[l]), row(pre_mlp_norm[l]),
                      w_up[l].astype(BF16), w_down[l].astype(BF16), row(post_mlp_norm[l]))
    return h.reshape(batch, seq, D_MODEL)
```

```python
import functools
import math

import jax
import jax.numpy as jnp
from jax import lax
from jax.experimental import pallas as pl
from jax.experimental.pallas import tpu as pltpu

F32 = jnp.float32
BF16 = jnp.bfloat16

D_MODEL = 1024
CHUNK = 64
EPS = 1e-6
MLA_HEADS = 8
QK_NOPE_DIM = 64
QK_ROPE_DIM = 32
V_HEAD_DIM = 64
Q_LORA_RANK = 768
KV_LORA_RANK = 256
ROPE_THETA = 10000.0
MLA_WIDTH = MLA_HEADS * V_HEAD_DIM
SSD_HEADS = 8
SSD_HEAD_DIM = 64
SSD_INNER = SSD_HEADS * SSD_HEAD_DIM
SSD_GROUPS = 2
SSD_STATE = 128
CONV_WIDTH = 4
CONV_DIM = SSD_INNER + 2 * SSD_GROUPS * SSD_STATE
D_FF = 4 * D_MODEL

LANES = 128
SUBLANES = 8
HEAD_PAD = LANES
HALF_ROPE = QK_ROPE_DIM // 2
ROPE_LO = QK_NOPE_DIM
ROPE_MID = QK_NOPE_DIM + HALF_ROPE
ROPE_HI = QK_NOPE_DIM + QK_ROPE_DIM
HEADS_PER_GROUP = SSD_HEADS // SSD_GROUPS
GROUP_WIDTH = HEADS_PER_GROUP * SSD_HEAD_DIM

C_Q0, C_Q1 = 0, Q_LORA_RANK
C_KV0, C_KV1 = C_Q1, C_Q1 + KV_LORA_RANK
C_KR0, C_KR1 = C_KV1, C_KV1 + HEAD_PAD
C_Z0, C_Z1 = C_KR1, C_KR1 + SSD_INNER
C_X0, C_X1 = C_Z1, C_Z1 + CONV_DIM
PROJ_COLS = C_X1
DT_LANE0 = ROPE_HI

BF16_SUBLANES = 2 * SUBLANES
VT_ROWS = V_HEAD_DIM + BF16_SUBLANES

ATTN_ROWS = 512
ATTN_HEADS = 4
PROJ_ROWS = ATTN_ROWS
SSD_ROWS = 512
MLP_ROWS = 512
FF_BLOCK = 1024
VMEM_LIMIT = 56 * 1024 * 1024


def _rms(x, w):
    return x * lax.rsqrt(jnp.mean(x * x, axis=-1, keepdims=True) + EPS) * w


def _const_spec(shape):
    zeros = (0,) * len(shape)
    return pl.BlockSpec(shape, lambda *_: zeros, pipeline_mode=pl.Buffered(1))


def _proj_body(h_ref, pos_ref, invf_ref, prew_ref, w1_ref, qnw_ref, wuq_ref, kvnw_ref, wuk_ref, wuv_ref,
               q_ref, k_ref, vt_ref, z_ref, xbc_ref, dt_ref):
    u = _rms(h_ref[...], prew_ref[...]).astype(BF16)

    def seg(c0, c1):
        return jnp.dot(u, w1_ref[:, c0:c1], preferred_element_type=F32)

    c_q = seg(C_Q0, C_Q1)
    c_kv = seg(C_KV0, C_KV1)
    kr_dt = seg(C_KR0, C_KR1)
    z_ref[...] = seg(C_Z0, C_Z1)
    xbc_ref[...] = seg(C_X0, C_X1)
    dt_ref[...] = kr_dt

    lane = lax.broadcasted_iota(jnp.int32, (1, LANES), 1)
    k_rope = jnp.where(lane < ROPE_HI, kr_dt, 0.0)
    first_half = (lane >= ROPE_LO) & (lane < ROPE_MID)
    second_half = (lane >= ROPE_MID) & (lane < ROPE_HI)
    ang = pos_ref[...].astype(F32) * invf_ref[...]
    cos = jnp.where(first_half | second_half, jnp.cos(ang), 1.0)
    sin = jnp.sin(ang)
    sin_first = jnp.where(first_half, -sin, 0.0)
    sin_second = jnp.where(second_half, sin, 0.0)

    def rope(t):
        return (t * cos + pltpu.roll(t, LANES - HALF_ROPE, 1) * sin_first
                + pltpu.roll(t, HALF_ROPE, 1) * sin_second)

    scale = (QK_NOPE_DIM + QK_ROPE_DIM) ** -0.5 * math.log2(math.e)
    qf = jnp.dot(_rms(c_q, qnw_ref[...]).astype(BF16), wuq_ref[...], preferred_element_type=F32)
    c_kv_n = _rms(c_kv, kvnw_ref[...]).astype(BF16)
    kf = jnp.dot(c_kv_n, wuk_ref[...], preferred_element_type=F32)
    k_rope = rope(k_rope)
    for h in range(MLA_HEADS):
        blk = slice(h * HEAD_PAD, (h + 1) * HEAD_PAD)
        q_ref[h] = (rope(qf[:, blk]) * scale).astype(BF16)
        k_ref[h] = (kf[:, blk] + k_rope).astype(BF16)
    v_t = jnp.dot(c_kv_n, wuv_ref[...], preferred_element_type=F32).T
    ones = jnp.ones((BF16_SUBLANES, v_t.shape[1]), BF16)
    for h in range(MLA_HEADS):
        vt_ref[h, 0, 0:V_HEAD_DIM, :] = v_t[h * V_HEAD_DIM:(h + 1) * V_HEAD_DIM, :].astype(BF16)
        vt_ref[h, 0, V_HEAD_DIM:VT_ROWS, :] = ones


def _proj_call(h, pos_b, invf, prew, w1, qnw, wuq, kvnw, wuk, wuv):
    T = h.shape[0]
    rows = PROJ_ROWS
    row_spec = lambda cols: pl.BlockSpec((rows, cols), lambda i: (i, 0))
    head_spec = pl.BlockSpec((MLA_HEADS, rows, HEAD_PAD), lambda i: (0, i, 0))
    return pl.pallas_call(
        _proj_body,
        grid=(T // rows,),
        in_specs=[row_spec(D_MODEL), row_spec(LANES), _const_spec((1, LANES)), _const_spec((1, D_MODEL)),
                  _const_spec((D_MODEL, PROJ_COLS)), _const_spec((1, Q_LORA_RANK)),
                  _const_spec((Q_LORA_RANK, MLA_HEADS * HEAD_PAD)), _const_spec((1, KV_LORA_RANK)),
                  _const_spec((KV_LORA_RANK, MLA_HEADS * HEAD_PAD)), _const_spec((KV_LORA_RANK, MLA_WIDTH))],
        out_specs=[head_spec, head_spec,
                   pl.BlockSpec((MLA_HEADS, 1, VT_ROWS, rows), lambda i: (0, i, 0, 0)),
                   row_spec(SSD_INNER), row_spec(CONV_DIM), row_spec(LANES)],
        out_shape=[jax.ShapeDtypeStruct((MLA_HEADS, T, HEAD_PAD), BF16),
                   jax.ShapeDtypeStruct((MLA_HEADS, T, HEAD_PAD), BF16),
                   jax.ShapeDtypeStruct((MLA_HEADS, T // rows, VT_ROWS, rows), BF16),
                   jax.ShapeDtypeStruct((T, SSD_INNER), F32),
                   jax.ShapeDtypeStruct((T, CONV_DIM), F32),
                   jax.ShapeDtypeStruct((T, LANES), F32)],
        compiler_params=pltpu.CompilerParams(dimension_semantics=("parallel",), vmem_limit_bytes=VMEM_LIMIT),
        name="proj",
    )(h, pos_b, invf, prew, w1, qnw, wuq, kvnw, wuk, wuv)


def _attn_body(q_ref, k_ref, vt_ref, bias_ref, o_ref, s0_scr, s1_scr, tmax_scr, m_scr, acc_scr, *, rows):
    qi = pl.program_id(2)
    s_scr = (s0_scr, s1_scr)

    def scores(hh, j, slot):
        start = pl.multiple_of(j * rows, rows)
        k = k_ref[hh, pl.ds(start, rows), :]
        s_t = lax.dot_general(k, q_ref[hh], (((1,), (1,)), ((), ())), preferred_element_type=F32)
        s_scr[slot][hh] = s_t
        tmax_scr[slot, hh] = jnp.max(s_t, axis=0, keepdims=True)

    def update(hh, j, slot, diagonal=False):
        s_t = s_scr[slot][hh]
        if diagonal:
            s_t = s_t + bias_ref[...]
            tile_max = jnp.max(s_t, axis=0, keepdims=True)
        else:
            tile_max = tmax_scr[slot, hh]
        m_prev = m_scr[hh]
        m_new = jnp.maximum(m_prev, tile_max)
        p_t = jnp.exp2(s_t - m_new).astype(BF16)
        contrib = jnp.dot(vt_ref[hh, j], p_t, preferred_element_type=F32)
        acc_scr[hh] = acc_scr[hh] * jnp.exp2(m_prev - m_new) + contrib
        m_scr[hh] = m_new

    m_scr[...] = jnp.full_like(m_scr, -jnp.inf)
    acc_scr[...] = jnp.zeros_like(acc_scr)
    for hh in range(ATTN_HEADS):
        scores(hh, 0, 0)

    def full_tile(j, carry):
        for slot in range(2):
            @pl.when(j % 2 == slot)
            def _():
                for hh in range(ATTN_HEADS):
                    scores(hh, j + 1, 1 - slot)
                    update(hh, j, slot)
        return carry

    lax.fori_loop(0, qi, full_tile, 0)
    for slot in range(2):
        @pl.when(qi % 2 == slot)
        def _():
            for hh in range(ATTN_HEADS):
                update(hh, qi, slot, diagonal=True)

    out_t = [acc_scr[hh, 0:V_HEAD_DIM, :] / acc_scr[hh, V_HEAD_DIM:V_HEAD_DIM + 1, :]
             for hh in range(ATTN_HEADS)]
    o_ref[...] = jnp.concatenate(out_t, axis=0).T.astype(o_ref.dtype)


def _attn_call(q, k, vt, batch, seq):
    rows = ATTN_ROWS
    n_q = seq // rows
    groups = MLA_HEADS // ATTN_HEADS
    chunk_of = jnp.arange(rows, dtype=jnp.int32) // CHUNK
    bias = jnp.where(chunk_of[:, None] <= chunk_of[None, :], 0.0, -jnp.inf).astype(F32)
    return pl.pallas_call(
        functools.partial(_attn_body, rows=rows),
        grid=(batch, groups, n_q),
        in_specs=[pl.BlockSpec((ATTN_HEADS, rows, HEAD_PAD), lambda b, g, i: (g, b * n_q + i, 0)),
                  pl.BlockSpec((ATTN_HEADS, seq, HEAD_PAD), lambda b, g, i: (g, b, 0)),
                  pl.BlockSpec((ATTN_HEADS, n_q, VT_ROWS, rows), lambda b, g, i: (g, b, 0, 0)),
                  _const_spec((rows, rows))],
        out_specs=pl.BlockSpec((rows, ATTN_HEADS * V_HEAD_DIM), lambda b, g, i: (b * n_q + i, g)),
        out_shape=jax.ShapeDtypeStruct((batch * seq, MLA_WIDTH), BF16),
        scratch_shapes=[pltpu.VMEM((ATTN_HEADS, rows, rows), F32), pltpu.VMEM((ATTN_HEADS, rows, rows), F32),
                        pltpu.VMEM((2, ATTN_HEADS, 1, rows), F32), pltpu.VMEM((ATTN_HEADS, 1, rows), F32),
                        pltpu.VMEM((ATTN_HEADS, VT_ROWS, rows), F32)],
        compiler_params=pltpu.CompilerParams(dimension_semantics=("parallel", "parallel", "arbitrary"),
                                             vmem_limit_bytes=VMEM_LIMIT),
        name="attn",
    )(q, k, vt, bias)


def _ssd_body(z_ref, xbc_ref, dt_ref, convw_ref, convb_ref, dtb_ref, expand_ref, alog_ref, dskip_ref, normw_ref,
              o_ref, ext_scr, act_scr, dts_scr, y_scr, state_scr, *, rows):
    tail = SUBLANES
    L = CHUNK

    @pl.when(pl.program_id(1) == 0)
    def _():
        ext_scr[0:tail, :] = jnp.zeros((tail, CONV_DIM), F32)
        state_scr[...] = jnp.zeros_like(state_scr)

    ext_scr[tail:tail + rows, :] = xbc_ref[...]
    conv = ext_scr[tail:tail + rows, :] * convw_ref[CONV_WIDTH - 1:CONV_WIDTH, :]
    for back in range(1, CONV_WIDTH):
        w_row = convw_ref[CONV_WIDTH - 1 - back:CONV_WIDTH - back, :]
        conv = conv + ext_scr[tail - back:tail - back + rows, :] * w_row
    conv = conv + convb_ref[...]
    act_scr[...] = conv * jax.nn.sigmoid(conv)
    ext_scr[0:tail, :] = ext_scr[rows:rows + tail, :]
    dt_heads = jax.nn.softplus(dt_ref[...] + dtb_ref[...])
    dt_hi = dt_heads.astype(BF16)
    rest = dt_heads - dt_hi.astype(F32)
    dt_mid = rest.astype(BF16)
    dt_lo = (rest - dt_mid.astype(F32)).astype(BF16)
    dts_scr[...] = jnp.dot(jnp.concatenate([dt_hi, dt_mid, dt_lo], axis=1), expand_ref[...],
                           preferred_element_type=F32)

    a_coef = -jnp.exp(alog_ref[...])
    row_i = lax.broadcasted_iota(jnp.int32, (L, SSD_INNER), 0)
    col_j = lax.broadcasted_iota(jnp.int32, (L, SSD_INNER), 1) % SSD_HEAD_DIM
    upper = (row_i <= col_j).astype(F32)
    lower = col_j <= row_i
    tril = (lax.broadcasted_iota(jnp.int32, (L, L), 1) <= lax.broadcasted_iota(jnp.int32, (L, L), 0)).astype(F32)
    blk_r = lax.broadcasted_iota(jnp.int32, (GROUP_WIDTH, GROUP_WIDTH), 0) // SSD_HEAD_DIM
    blk_c = lax.broadcasted_iota(jnp.int32, (GROUP_WIDTH, GROUP_WIDTH), 1) // SSD_HEAD_DIM
    same_head = blk_r == blk_c

    def chunk(c, carry):
        r0 = c * L
        xs = act_scr[pl.ds(r0, L), 0:SSD_INNER]
        dt = dts_scr[pl.ds(r0, L), :]
        a = dt * a_coef
        xdt = xs * dt
        a_cs = jnp.dot(tril, a, preferred_element_type=F32, precision=lax.Precision.HIGHEST)
        a_cs_row = jnp.sum(a * upper, axis=0, keepdims=True)
        decay_in = jnp.exp(jnp.where(lower, a_cs - a_cs_row, -jnp.inf))
        a_last = a_cs[L - 1:L, :]
        decay_to_end = jnp.exp(a_last - a_cs)
        decay_from_start = jnp.exp(a_cs)
        chunk_decay = jnp.exp(a_last)
        for g in range(SSD_GROUPS):
            gl = slice(g * GROUP_WIDTH, (g + 1) * GROUP_WIDTH)
            b_off = SSD_INNER + g * SSD_STATE
            c_off = SSD_INNER + SSD_GROUPS * SSD_STATE + g * SSD_STATE
            b_f32 = act_scr[pl.ds(r0, L), b_off:b_off + SSD_STATE]
            b_g = b_f32.astype(BF16)
            c_g = act_scr[pl.ds(r0, L), c_off:c_off + SSD_STATE].astype(BF16)
            b_rep = jnp.concatenate([b_g] * HEADS_PER_GROUP, axis=0)
            cb = lax.dot_general(c_g, b_rep, (((1,), (1,)), ((), ())), preferred_element_type=F32)
            scores = (cb * decay_in[:, gl]).astype(BF16)
            x_g = xdt[:, gl]
            x_rep = jnp.concatenate([x_g] * HEADS_PER_GROUP, axis=0)
            x_diag = jnp.where(same_head, x_rep, 0.0).astype(BF16)
            y_diag = jnp.dot(scores, x_diag, preferred_element_type=F32)
            state = state_scr[g]
            y_off = jnp.dot(c_g, state.astype(BF16), preferred_element_type=F32) * decay_from_start[:, gl]
            x_dec = (x_g * decay_to_end[:, gl]).astype(BF16)
            new_state = jnp.dot(b_f32.T.astype(BF16), x_dec, preferred_element_type=F32)
            state_scr[g] = state * chunk_decay[:, gl] + new_state
            y_scr[pl.ds(r0, L), gl] = y_diag + y_off + dskip_ref[:, gl] * xs[:, gl]
        return carry

    for c in range(rows // L):
        chunk(c, 0)

    zf = z_ref[...]
    y = y_scr[...] * (zf * jax.nn.sigmoid(zf))
    for g in range(SSD_GROUPS):
        gl = slice(g * GROUP_WIDTH, (g + 1) * GROUP_WIDTH)
        yg = y[:, gl]
        yg = yg * lax.rsqrt(jnp.mean(yg * yg, axis=-1, keepdims=True) + EPS)
        o_ref[:, gl] = (yg * normw_ref[:, gl]).astype(o_ref.dtype)


def _ssd_call(z, xbc, dt, convw, convb, dtb, alog, dskip, normw, batch, seq):
    rows = SSD_ROWS
    n_t = seq // rows
    row_spec = lambda cols: pl.BlockSpec((rows, cols), lambda b, i: (b * n_t + i, 0))
    src_head = jnp.arange(LANES, dtype=jnp.int32)[:, None] - DT_LANE0
    dst_head = jnp.arange(SSD_INNER, dtype=jnp.int32)[None, :] // SSD_HEAD_DIM
    expand = jnp.tile((src_head == dst_head).astype(BF16), (3, 1))
    return pl.pallas_call(
        functools.partial(_ssd_body, rows=rows),
        grid=(batch, n_t),
        in_specs=[row_spec(SSD_INNER), row_spec(CONV_DIM), row_spec(LANES),
                  _const_spec((CONV_WIDTH, CONV_DIM)), _const_spec((1, CONV_DIM)), _const_spec((1, LANES)),
                  _const_spec((3 * LANES, SSD_INNER)),
                  _const_spec((1, SSD_INNER)), _const_spec((1, SSD_INNER)), _const_spec((1, SSD_INNER))],
        out_specs=row_spec(SSD_INNER),
        out_shape=jax.ShapeDtypeStruct((batch * seq, SSD_INNER), BF16),
        scratch_shapes=[pltpu.VMEM((rows + SUBLANES, CONV_DIM), F32), pltpu.VMEM((rows, CONV_DIM), F32),
                        pltpu.VMEM((rows, SSD_INNER), F32), pltpu.VMEM((rows, SSD_INNER), F32),
                        pltpu.VMEM((SSD_GROUPS, SSD_STATE, GROUP_WIDTH), F32)],
        compiler_params=pltpu.CompilerParams(dimension_semantics=("arbitrary", "arbitrary"),
                                             vmem_limit_bytes=VMEM_LIMIT),
        name="ssd",
    )(z, xbc, dt, convw, convb, dtb, expand, alog, dskip, normw)


def _mlp_body(h_ref, ya_ref, ys_ref, wout_ref, postmix_ref, premlp_ref, wup_ref, wdn_ref, postmlp_ref, o_ref):
    mixed = (jnp.dot(ya_ref[...], wout_ref[0:MLA_WIDTH, :], preferred_element_type=F32)
             + jnp.dot(ys_ref[...], wout_ref[MLA_WIDTH:, :], preferred_element_type=F32))
    h1 = h_ref[...] + _rms(mixed, postmix_ref[...])
    m = _rms(h1, premlp_ref[...]).astype(BF16)
    acc = None
    for c in range(D_FF // FF_BLOCK):
        ff = slice(c * FF_BLOCK, (c + 1) * FF_BLOCK)
        up = jnp.dot(m, wup_ref[:, ff], preferred_element_type=F32)
        act = jnp.square(jnp.maximum(up, 0.0)).astype(BF16)
        part = jnp.dot(act, wdn_ref[ff, :], preferred_element_type=F32)
        acc = part if acc is None else acc + part
    o_ref[...] = h1 + _rms(acc, postmlp_ref[...])


def _mlp_call(h, ya, ys, wout, postmix, premlp, wup, wdn, postmlp):
    T = h.shape[0]
    rows = MLP_ROWS
    row_spec = lambda cols: pl.BlockSpec((rows, cols), lambda i: (i, 0))
    return pl.pallas_call(
        _mlp_body,
        grid=(T // rows,),
        in_specs=[row_spec(D_MODEL), row_spec(MLA_WIDTH), row_spec(SSD_INNER),
                  _const_spec((MLA_WIDTH + SSD_INNER, D_MODEL)), _const_spec((1, D_MODEL)),
                  _const_spec((1, D_MODEL)), _const_spec((D_MODEL, D_FF)), _const_spec((D_FF, D_MODEL)),
                  _const_spec((1, D_MODEL))],
        out_specs=row_spec(D_MODEL),
        out_shape=jax.ShapeDtypeStruct((T, D_MODEL), F32),
        compiler_params=pltpu.CompilerParams(dimension_semantics=("parallel",), vmem_limit_bytes=VMEM_LIMIT),
        name="mlp",
    )(h, ya, ys, wout, postmix, premlp, wup, wdn, postmlp)


def _pack_w_in(w_in):
    s1 = Q_LORA_RANK
    s2 = s1 + KV_LORA_RANK
    s3 = s2 + QK_ROPE_DIM
    s4 = s3 + SSD_INNER
    s5 = s4 + CONV_DIM
    zeros = lambda n: jnp.zeros((D_MODEL, n), w_in.dtype)
    kr_dt = jnp.concatenate([zeros(ROPE_LO), w_in[:, s2:s3], w_in[:, s5:],
                             zeros(HEAD_PAD - DT_LANE0 - SSD_HEADS)], axis=1)
    return jnp.concatenate([w_in[:, :s2], kr_dt, w_in[:, s3:s5]], axis=1).astype(BF16)


def _pack_w_uq(w_uq):
    w = w_uq.reshape(Q_LORA_RANK, MLA_HEADS, QK_NOPE_DIM + QK_ROPE_DIM)
    pad = jnp.zeros((Q_LORA_RANK, MLA_HEADS, HEAD_PAD - ROPE_HI), w_uq.dtype)
    return jnp.concatenate([w, pad], axis=-1).reshape(Q_LORA_RANK, MLA_HEADS * HEAD_PAD).astype(BF16)


def _pack_w_ukv(w_ukv):
    w = w_ukv.reshape(KV_LORA_RANK, MLA_HEADS, QK_NOPE_DIM + V_HEAD_DIM)
    pad = jnp.zeros((KV_LORA_RANK, MLA_HEADS, HEAD_PAD - QK_NOPE_DIM), w_ukv.dtype)
    w_uk = jnp.concatenate([w[..., :QK_NOPE_DIM], pad], axis=-1).reshape(KV_LORA_RANK, MLA_HEADS * HEAD_PAD)
    w_uv = w[..., QK_NOPE_DIM:].reshape(KV_LORA_RANK, MLA_WIDTH)
    return w_uk.astype(BF16), w_uv.astype(BF16)


def _per_lane(v):
    return jnp.repeat(v.astype(F32), SSD_HEAD_DIM)[None, :]


def kernel(x, positions, pre_mix_norm, w_in, q_norm, w_uq, kv_norm, w_ukv, conv_w, conv_b, dt_bias, a_log,
           d_skip, ssd_norm, w_out, post_mix_norm, pre_mlp_norm, w_up, w_down, post_mlp_norm):
    batch, seq, _ = x.shape
    depth = w_in.shape[0]
    T = batch * seq
    assert seq % ATTN_ROWS == 0 and seq % SSD_ROWS == 0 and T % PROJ_ROWS == 0 and T % MLP_ROWS == 0

    inv_freq = ROPE_THETA ** (-jnp.arange(0, QK_ROPE_DIM, 2, dtype=F32) / QK_ROPE_DIM)
    invf = jnp.concatenate([jnp.zeros((ROPE_LO,), F32), inv_freq, inv_freq,
                            jnp.zeros((HEAD_PAD - ROPE_HI,), F32)])[None, :]
    pos_b = jnp.broadcast_to(positions.reshape(T, 1), (T, LANES))
    row = lambda v: v.astype(F32)[None, :]

    h = x.reshape(T, D_MODEL)
    for l in range(depth):
        w_uk, w_uv = _pack_w_ukv(w_ukv[l])
        q, k, vt, z, xbc, dt = _proj_call(h, pos_b, invf, row(pre_mix_norm[l]), _pack_w_in(w_in[l]),
                                         row(q_norm[l]), _pack_w_uq(w_uq[l]), row(kv_norm[l]), w_uk, w_uv)
        y_att = _attn_call(q, k, vt, batch, seq)
        dt_bias_lanes = jnp.zeros((1, LANES), F32).at[0, DT_LANE0:DT_LANE0 + SSD_HEADS].set(
            dt_bias[l].astype(F32))
        y_ssd = _ssd_call(z, xbc, dt, conv_w[l].astype(F32), row(conv_b[l]), dt_bias_lanes,
                          _per_lane(a_log[l]), _per_lane(d_skip[l]), row(ssd_norm[l]), batch, seq)
        h = _mlp_call(h, y_att, y_ssd, w_out[l].astype(BF16), row(post_mix_norm[l]), row(pre_mlp_norm[l]),
                      w_up[l].astype(BF16), w_down[l].astype(BF16), row(post_mlp_norm[l]))
    return h.reshape(batch, seq, D_MODEL)
```

```python
import functools
import math

import jax
import jax.numpy as jnp
from jax import lax
from jax.experimental import pallas as pl
from jax.experimental.pallas import tpu as pltpu

F32 = jnp.float32
BF16 = jnp.bfloat16

D_MODEL = 1024
CHUNK = 64
EPS = 1e-6
MLA_HEADS = 8
QK_NOPE_DIM = 64
QK_ROPE_DIM = 32
V_HEAD_DIM = 64
Q_LORA_RANK = 768
KV_LORA_RANK = 256
ROPE_THETA = 10000.0
MLA_WIDTH = MLA_HEADS * V_HEAD_DIM
SSD_HEADS = 8
SSD_HEAD_DIM = 64
SSD_INNER = SSD_HEADS * SSD_HEAD_DIM
SSD_GROUPS = 2
SSD_STATE = 128
CONV_WIDTH = 4
CONV_DIM = SSD_INNER + 2 * SSD_GROUPS * SSD_STATE
D_FF = 4 * D_MODEL

LANES = 128
SUBLANES = 8
HEAD_PAD = LANES
HALF_ROPE = QK_ROPE_DIM // 2
ROPE_LO = QK_NOPE_DIM
ROPE_MID = QK_NOPE_DIM + HALF_ROPE
ROPE_HI = QK_NOPE_DIM + QK_ROPE_DIM
HEADS_PER_GROUP = SSD_HEADS // SSD_GROUPS
GROUP_WIDTH = HEADS_PER_GROUP * SSD_HEAD_DIM

C_Q0, C_Q1 = 0, Q_LORA_RANK
C_KV0, C_KV1 = C_Q1, C_Q1 + KV_LORA_RANK
C_KR0, C_KR1 = C_KV1, C_KV1 + HEAD_PAD
C_Z0, C_Z1 = C_KR1, C_KR1 + SSD_INNER
C_X0, C_X1 = C_Z1, C_Z1 + CONV_DIM
PROJ_COLS = C_X1
DT_LANE0 = ROPE_HI

BF16_SUBLANES = 2 * SUBLANES
VT_ROWS = V_HEAD_DIM + BF16_SUBLANES

ATTN_ROWS = 512
ATTN_HEADS = 4
PROJ_ROWS = ATTN_ROWS
SSD_ROWS = 512
MLP_ROWS = 512
FF_BLOCK = 1024
VMEM_LIMIT = 56 * 1024 * 1024


def _rms(x, w):
    return x * lax.rsqrt(jnp.mean(x * x, axis=-1, keepdims=True) + EPS) * w


def _const_spec(shape):
    zeros = (0,) * len(shape)
    return pl.BlockSpec(shape, lambda *_: zeros, pipeline_mode=pl.Buffered(1))


def _proj_body(h_ref, pos_ref, invf_ref, prew_ref, w1_ref, qnw_ref, wuq_ref, kvnw_ref, wuk_ref, wuv_ref,
               q_ref, k_ref, vt_ref, z_ref, xbc_ref, dt_ref):
    u = _rms(h_ref[...], prew_ref[...]).astype(BF16)

    def seg(c0, c1):
        return jnp.dot(u, w1_ref[:, c0:c1], preferred_element_type=F32)

    c_q = seg(C_Q0, C_Q1)
    c_kv = seg(C_KV0, C_KV1)
    kr_dt = seg(C_KR0, C_KR1)
    z_ref[...] = seg(C_Z0, C_Z1)
    xbc_ref[...] = seg(C_X0, C_X1)
    dt_ref[...] = kr_dt

    lane = lax.broadcasted_iota(jnp.int32, (1, LANES), 1)
    k_rope = jnp.where(lane < ROPE_HI, kr_dt, 0.0)
    first_half = (lane >= ROPE_LO) & (lane < ROPE_MID)
    second_half = (lane >= ROPE_MID) & (lane < ROPE_HI)
    ang = pos_ref[...].astype(F32) * invf_ref[...]
    cos = jnp.where(first_half | second_half, jnp.cos(ang), 1.0)
    sin = jnp.sin(ang)
    sin_first = jnp.where(first_half, -sin, 0.0)
    sin_second = jnp.where(second_half, sin, 0.0)

    def rope(t):
        return (t * cos + pltpu.roll(t, LANES - HALF_ROPE, 1) * sin_first
                + pltpu.roll(t, HALF_ROPE, 1) * sin_second)

    scale = (QK_NOPE_DIM + QK_ROPE_DIM) ** -0.5 * math.log2(math.e)
    qf = jnp.dot(_rms(c_q, qnw_ref[...]).astype(BF16), wuq_ref[...], preferred_element_type=F32)
    c_kv_n = _rms(c_kv, kvnw_ref[...]).astype(BF16)
    kf = jnp.dot(c_kv_n, wuk_ref[...], preferred_element_type=F32)
    k_rope = rope(k_rope)
    for h in range(MLA_HEADS):
        blk = slice(h * HEAD_PAD, (h + 1) * HEAD_PAD)
        q_ref[h] = (rope(qf[:, blk]) * scale).astype(BF16)
        k_ref[h] = (kf[:, blk] + k_rope).astype(BF16)
    v_t = jnp.dot(c_kv_n, wuv_ref[...], preferred_element_type=F32).T
    ones = jnp.ones((BF16_SUBLANES, v_t.shape[1]), BF16)
    for h in range(MLA_HEADS):
        vt_ref[h, 0, 0:V_HEAD_DIM, :] = v_t[h * V_HEAD_DIM:(h + 1) * V_HEAD_DIM, :].astype(BF16)
        vt_ref[h, 0, V_HEAD_DIM:VT_ROWS, :] = ones


def _proj_call(h, pos_b, invf, prew, w1, qnw, wuq, kvnw, wuk, wuv):
    T = h.shape[0]
    rows = PROJ_ROWS
    row_spec = lambda cols: pl.BlockSpec((rows, cols), lambda i: (i, 0))
    head_spec = pl.BlockSpec((MLA_HEADS, rows, HEAD_PAD), lambda i: (0, i, 0))
    return pl.pallas_call(
        _proj_body,
        grid=(T // rows,),
        in_specs=[row_spec(D_MODEL), row_spec(LANES), _const_spec((1, LANES)), _const_spec((1, D_MODEL)),
                  _const_spec((D_MODEL, PROJ_COLS)), _const_spec((1, Q_LORA_RANK)),
                  _const_spec((Q_LORA_RANK, MLA_HEADS * HEAD_PAD)), _const_spec((1, KV_LORA_RANK)),
                  _const_spec((KV_LORA_RANK, MLA_HEADS * HEAD_PAD)), _const_spec((KV_LORA_RANK, MLA_WIDTH))],
        out_specs=[head_spec, head_spec,
                   pl.BlockSpec((MLA_HEADS, 1, VT_ROWS, rows), lambda i: (0, i, 0, 0)),
                   row_spec(SSD_INNER), row_spec(CONV_DIM), row_spec(LANES)],
        out_shape=[jax.ShapeDtypeStruct((MLA_HEADS, T, HEAD_PAD), BF16),
                   jax.ShapeDtypeStruct((MLA_HEADS, T, HEAD_PAD), BF16),
                   jax.ShapeDtypeStruct((MLA_HEADS, T // rows, VT_ROWS, rows), BF16),
                   jax.ShapeDtypeStruct((T, SSD_INNER), F32),
                   jax.ShapeDtypeStruct((T, CONV_DIM), F32),
                   jax.ShapeDtypeStruct((T, LANES), F32)],
        compiler_params=pltpu.CompilerParams(dimension_semantics=("parallel",), vmem_limit_bytes=VMEM_LIMIT),
        name="proj",
    )(h, pos_b, invf, prew, w1, qnw, wuq, kvnw, wuk, wuv)


def _attn_body(q_ref, k_ref, vt_ref, bias_ref, o_ref, s0_scr, s1_scr, tmax_scr, m_scr, acc_scr, *, rows):
    qi = pl.program_id(2)
    s_scr = (s0_scr, s1_scr)

    def scores(hh, j, slot):
        start = pl.multiple_of(j * rows, rows)
        k = k_ref[hh, pl.ds(start, rows), :]
        s_t = lax.dot_general(k, q_ref[hh], (((1,), (1,)), ((), ())), preferred_element_type=F32)
        s_scr[slot][hh] = s_t
        tmax_scr[slot, hh] = jnp.max(s_t, axis=0, keepdims=True)

    def update(hh, j, slot, diagonal=False):
        s_t = s_scr[slot][hh]
        if diagonal:
            s_t = s_t + bias_ref[...]
            tile_max = jnp.max(s_t, axis=0, keepdims=True)
        else:
            tile_max = tmax_scr[slot, hh]
        m_prev = m_scr[hh]
        m_new = jnp.maximum(m_prev, tile_max)
        p_t = jnp.exp2(s_t - m_new).astype(BF16)
        contrib = jnp.dot(vt_ref[hh, j], p_t, preferred_element_type=F32)
        acc_scr[hh] = acc_scr[hh] * jnp.exp2(m_prev - m_new) + contrib
        m_scr[hh] = m_new

    m_scr[...] = jnp.full_like(m_scr, -jnp.inf)
    acc_scr[...] = jnp.zeros_like(acc_scr)
    for hh in range(ATTN_HEADS):
        scores(hh, 0, 0)

    def step(j, slot, diagonal=False):
        for hh in range(ATTN_HEADS):
            scores(hh, j + 1, 1 - slot)
            update(hh, j, slot, diagonal)

    def tile_pair(i, carry):
        step(2 * i, 0)
        step(2 * i + 1, 1)
        return carry

    lax.fori_loop(0, qi // 2, tile_pair, 0)

    @pl.when(qi % 2 == 0)
    def _():
        for hh in range(ATTN_HEADS):
            update(hh, qi, 0, diagonal=True)

    @pl.when(qi % 2 == 1)
    def _():
        step(qi - 1, 0)
        for hh in range(ATTN_HEADS):
            update(hh, qi, 1, diagonal=True)

    out_t = [acc_scr[hh, 0:V_HEAD_DIM, :] / acc_scr[hh, V_HEAD_DIM:V_HEAD_DIM + 1, :]
             for hh in range(ATTN_HEADS)]
    o_ref[...] = jnp.concatenate(out_t, axis=0).T.astype(o_ref.dtype)


def _attn_call(q, k, vt, batch, seq):
    rows = ATTN_ROWS
    n_q = seq // rows
    groups = MLA_HEADS // ATTN_HEADS
    chunk_of = jnp.arange(rows, dtype=jnp.int32) // CHUNK
    bias = jnp.where(chunk_of[:, None] <= chunk_of[None, :], 0.0, -jnp.inf).astype(F32)
    return pl.pallas_call(
        functools.partial(_attn_body, rows=rows),
        grid=(batch, groups, n_q),
        in_specs=[pl.BlockSpec((ATTN_HEADS, rows, HEAD_PAD), lambda b, g, i: (g, b * n_q + i, 0)),
                  pl.BlockSpec((ATTN_HEADS, seq, HEAD_PAD), lambda b, g, i: (g, b, 0)),
                  pl.BlockSpec((ATTN_HEADS, n_q, VT_ROWS, rows), lambda b, g, i: (g, b, 0, 0)),
                  _const_spec((rows, rows))],
        out_specs=pl.BlockSpec((rows, ATTN_HEADS * V_HEAD_DIM), lambda b, g, i: (b * n_q + i, g)),
        out_shape=jax.ShapeDtypeStruct((batch * seq, MLA_WIDTH), BF16),
        scratch_shapes=[pltpu.VMEM((ATTN_HEADS, rows, rows), F32), pltpu.VMEM((ATTN_HEADS, rows, rows), F32),
                        pltpu.VMEM((2, ATTN_HEADS, 1, rows), F32), pltpu.VMEM((ATTN_HEADS, 1, rows), F32),
                        pltpu.VMEM((ATTN_HEADS, VT_ROWS, rows), F32)],
        compiler_params=pltpu.CompilerParams(dimension_semantics=("parallel", "parallel", "arbitrary"),
                                             vmem_limit_bytes=VMEM_LIMIT),
        name="attn",
    )(q, k, vt, bias)


def _ssd_body(z_ref, xbc_ref, dt_ref, convw_ref, convb_ref, dtb_ref, expand_ref, alog_ref, dskip_ref, normw_ref,
              o_ref, ext_scr, act_scr, dts_scr, y_scr, state_scr, *, rows):
    tail = SUBLANES
    L = CHUNK

    @pl.when(pl.program_id(1) == 0)
    def _():
        ext_scr[0:tail, :] = jnp.zeros((tail, CONV_DIM), F32)
        state_scr[...] = jnp.zeros_like(state_scr)

    ext_scr[tail:tail + rows, :] = xbc_ref[...]
    conv = ext_scr[tail:tail + rows, :] * convw_ref[CONV_WIDTH - 1:CONV_WIDTH, :]
    for back in range(1, CONV_WIDTH):
        w_row = convw_ref[CONV_WIDTH - 1 - back:CONV_WIDTH - back, :]
        conv = conv + ext_scr[tail - back:tail - back + rows, :] * w_row
    conv = conv + convb_ref[...]
    act_scr[...] = conv * jax.nn.sigmoid(conv)
    ext_scr[0:tail, :] = ext_scr[rows:rows + tail, :]
    dt_heads = jax.nn.softplus(dt_ref[...] + dtb_ref[...])
    dt_hi = dt_heads.astype(BF16)
    rest = dt_heads - dt_hi.astype(F32)
    dt_mid = rest.astype(BF16)
    dt_lo = (rest - dt_mid.astype(F32)).astype(BF16)
    dts_scr[...] = jnp.dot(jnp.concatenate([dt_hi, dt_mid, dt_lo], axis=1), expand_ref[...],
                           preferred_element_type=F32)

    a_coef = -jnp.exp(alog_ref[...])
    row_i = lax.broadcasted_iota(jnp.int32, (L, SSD_INNER), 0)
    col_j = lax.broadcasted_iota(jnp.int32, (L, SSD_INNER), 1) % SSD_HEAD_DIM
    upper = (row_i <= col_j).astype(F32)
    lower = col_j <= row_i
    tril = (lax.broadcasted_iota(jnp.int32, (L, L), 1) <= lax.broadcasted_iota(jnp.int32, (L, L), 0)).astype(F32)
    blk_r = lax.broadcasted_iota(jnp.int32, (GROUP_WIDTH, GROUP_WIDTH), 0) // SSD_HEAD_DIM
    blk_c = lax.broadcasted_iota(jnp.int32, (GROUP_WIDTH, GROUP_WIDTH), 1) // SSD_HEAD_DIM
    same_head = blk_r == blk_c

    def chunk(c, carry):
        r0 = c * L
        xs = act_scr[pl.ds(r0, L), 0:SSD_INNER]
        dt = dts_scr[pl.ds(r0, L), :]
        a = dt * a_coef
        xdt = xs * dt
        a_cs = jnp.dot(tril, a, preferred_element_type=F32, precision=lax.Precision.HIGHEST)
        a_cs_row = jnp.sum(a * upper, axis=0, keepdims=True)
        decay_in = jnp.exp(jnp.where(lower, a_cs - a_cs_row, -jnp.inf))
        a_last = a_cs[L - 1:L, :]
        decay_to_end = jnp.exp(a_last - a_cs)
        decay_from_start = jnp.exp(a_cs)
        chunk_decay = jnp.exp(a_last)
        for g in range(SSD_GROUPS):
            gl = slice(g * GROUP_WIDTH, (g + 1) * GROUP_WIDTH)
            b_off = SSD_INNER + g * SSD_STATE
            c_off = SSD_INNER + SSD_GROUPS * SSD_STATE + g * SSD_STATE
            b_f32 = act_scr[pl.ds(r0, L), b_off:b_off + SSD_STATE]
            b_g = b_f32.astype(BF16)
            c_g = act_scr[pl.ds(r0, L), c_off:c_off + SSD_STATE].astype(BF16)
            b_rep = jnp.concatenate([b_g] * HEADS_PER_GROUP, axis=0)
            cb = lax.dot_general(c_g, b_rep, (((1,), (1,)), ((), ())), preferred_element_type=F32)
            scores = (cb * decay_in[:, gl]).astype(BF16)
            x_g = xdt[:, gl]
            x_rep = jnp.concatenate([x_g] * HEADS_PER_GROUP, axis=0)
            x_diag = jnp.where(same_head, x_rep, 0.0).astype(BF16)
            y_diag = jnp.dot(scores, x_diag, preferred_element_type=F32)
            state = state_scr[g]
            y_off = jnp.dot(c_g, state.astype(BF16), preferred_element_type=F32) * decay_from_start[:, gl]
            x_dec = (x_g * decay_to_end[:, gl]).astype(BF16)
            new_state = jnp.dot(b_f32.T.astype(BF16), x_dec, preferred_element_type=F32)
            state_scr[g] = state * chunk_decay[:, gl] + new_state
            y_scr[pl.ds(r0, L), gl] = y_diag + y_off + dskip_ref[:, gl] * xs[:, gl]
        return carry

    for c in range(rows // L):
        chunk(c, 0)

    zf = z_ref[...]
    y = y_scr[...] * (zf * jax.nn.sigmoid(zf))
    for g in range(SSD_GROUPS):
        gl = slice(g * GROUP_WIDTH, (g + 1) * GROUP_WIDTH)
        yg = y[:, gl]
        yg = yg * lax.rsqrt(jnp.mean(yg * yg, axis=-1, keepdims=True) + EPS)
        o_ref[:, gl] = (yg * normw_ref[:, gl]).astype(o_ref.dtype)


def _ssd_call(z, xbc, dt, convw, convb, dtb, alog, dskip, normw, batch, seq):
    rows = SSD_ROWS
    n_t = seq // rows
    row_spec = lambda cols: pl.BlockSpec((rows, cols), lambda b, i: (b * n_t + i, 0))
    src_head = jnp.arange(LANES, dtype=jnp.int32)[:, None] - DT_LANE0
    dst_head = jnp.arange(SSD_INNER, dtype=jnp.int32)[None, :] // SSD_HEAD_DIM
    expand = jnp.tile((src_head == dst_head).astype(BF16), (3, 1))
    return pl.pallas_call(
        functools.partial(_ssd_body, rows=rows),
        grid=(batch, n_t),
        in_specs=[row_spec(SSD_INNER), row_spec(CONV_DIM), row_spec(LANES),
                  _const_spec((CONV_WIDTH, CONV_DIM)), _const_spec((1, CONV_DIM)), _const_spec((1, LANES)),
                  _const_spec((3 * LANES, SSD_INNER)),
                  _const_spec((1, SSD_INNER)), _const_spec((1, SSD_INNER)), _const_spec((1, SSD_INNER))],
        out_specs=row_spec(SSD_INNER),
        out_shape=jax.ShapeDtypeStruct((batch * seq, SSD_INNER), BF16),
        scratch_shapes=[pltpu.VMEM((rows + SUBLANES, CONV_DIM), F32), pltpu.VMEM((rows, CONV_DIM), F32),
                        pltpu.VMEM((rows, SSD_INNER), F32), pltpu.VMEM((rows, SSD_INNER), F32),
                        pltpu.VMEM((SSD_GROUPS, SSD_STATE, GROUP_WIDTH), F32)],
        compiler_params=pltpu.CompilerParams(dimension_semantics=("arbitrary", "arbitrary"),
                                             vmem_limit_bytes=VMEM_LIMIT),
        name="ssd",
    )(z, xbc, dt, convw, convb, dtb, expand, alog, dskip, normw)


def _mlp_body(h_ref, ya_ref, ys_ref, wout_ref, postmix_ref, premlp_ref, wup_ref, wdn_ref, postmlp_ref, o_ref):
    mixed = (jnp.dot(ya_ref[...], wout_ref[0:MLA_WIDTH, :], preferred_element_type=F32)
             + jnp.dot(ys_ref[...], wout_ref[MLA_WIDTH:, :], preferred_element_type=F32))
    h1 = h_ref[...] + _rms(mixed, postmix_ref[...])
    m = _rms(h1, premlp_ref[...]).astype(BF16)
    acc = None
    for c in range(D_FF // FF_BLOCK):
        ff = slice(c * FF_BLOCK, (c + 1) * FF_BLOCK)
        up = jnp.dot(m, wup_ref[:, ff], preferred_element_type=F32)
        act = jnp.square(jnp.maximum(up, 0.0)).astype(BF16)
        part = jnp.dot(act, wdn_ref[ff, :], preferred_element_type=F32)
        acc = part if acc is None else acc + part
    o_ref[...] = h1 + _rms(acc, postmlp_ref[...])


def _mlp_call(h, ya, ys, wout, postmix, premlp, wup, wdn, postmlp):
    T = h.shape[0]
    rows = MLP_ROWS
    row_spec = lambda cols: pl.BlockSpec((rows, cols), lambda i: (i, 0))
    return pl.pallas_call(
        _mlp_body,
        grid=(T // rows,),
        in_specs=[row_spec(D_MODEL), row_spec(MLA_WIDTH), row_spec(SSD_INNER),
                  _const_spec((MLA_WIDTH + SSD_INNER, D_MODEL)), _const_spec((1, D_MODEL)),
                  _const_spec((1, D_MODEL)), _const_spec((D_MODEL, D_FF)), _const_spec((D_FF, D_MODEL)),
                  _const_spec((1, D_MODEL))],
        out_specs=row_spec(D_MODEL),
        out_shape=jax.ShapeDtypeStruct((T, D_MODEL), F32),
        compiler_params=pltpu.CompilerParams(dimension_semantics=("parallel",), vmem_limit_bytes=VMEM_LIMIT),
        name="mlp",
    )(h, ya, ys, wout, postmix, premlp, wup, wdn, postmlp)


def _pack_w_in(w_in):
    s1 = Q_LORA_RANK
    s2 = s1 + KV_LORA_RANK
    s3 = s2 + QK_ROPE_DIM
    s4 = s3 + SSD_INNER
    s5 = s4 + CONV_DIM
    zeros = lambda n: jnp.zeros((D_MODEL, n), w_in.dtype)
    kr_dt = jnp.concatenate([zeros(ROPE_LO), w_in[:, s2:s3], w_in[:, s5:],
                             zeros(HEAD_PAD - DT_LANE0 - SSD_HEADS)], axis=1)
    return jnp.concatenate([w_in[:, :s2], kr_dt, w_in[:, s3:s5]], axis=1).astype(BF16)


def _pack_w_uq(w_uq):
    w = w_uq.reshape(Q_LORA_RANK, MLA_HEADS, QK_NOPE_DIM + QK_ROPE_DIM)
    pad = jnp.zeros((Q_LORA_RANK, MLA_HEADS, HEAD_PAD - ROPE_HI), w_uq.dtype)
    return jnp.concatenate([w, pad], axis=-1).reshape(Q_LORA_RANK, MLA_HEADS * HEAD_PAD).astype(BF16)


def _pack_w_ukv(w_ukv):
    w = w_ukv.reshape(KV_LORA_RANK, MLA_HEADS, QK_NOPE_DIM + V_HEAD_DIM)
    pad = jnp.zeros((KV_LORA_RANK, MLA_HEADS, HEAD_PAD - QK_NOPE_DIM), w_ukv.dtype)
    w_uk = jnp.concatenate([w[..., :QK_NOPE_DIM], pad], axis=-1).reshape(KV_LORA_RANK, MLA_HEADS * HEAD_PAD)
    w_uv = w[..., QK_NOPE_DIM:].reshape(KV_LORA_RANK, MLA_WIDTH)
    return w_uk.astype(BF16), w_uv.astype(BF16)


def _per_lane(v):
    return jnp.repeat(v.astype(F32), SSD_HEAD_DIM)[None, :]


def kernel(x, positions, pre_mix_norm, w_in, q_norm, w_uq, kv_norm, w_ukv, conv_w, conv_b, dt_bias, a_log,
           d_skip, ssd_norm, w_out, post_mix_norm, pre_mlp_norm, w_up, w_down, post_mlp_norm):
    batch, seq, _ = x.shape
    depth = w_in.shape[0]
    T = batch * seq
    assert seq % ATTN_ROWS == 0 and seq % SSD_ROWS == 0 and T % PROJ_ROWS == 0 and T % MLP_ROWS == 0

    inv_freq = ROPE_THETA ** (-jnp.arange(0, QK_ROPE_DIM, 2, dtype=F32) / QK_ROPE_DIM)
    invf = jnp.concatenate([jnp.zeros((ROPE_LO,), F32), inv_freq, inv_freq,
                            jnp.zeros((HEAD_PAD - ROPE_HI,), F32)])[None, :]
    pos_b = jnp.broadcast_to(positions.reshape(T, 1), (T, LANES))
    row = lambda v: v.astype(F32)[None, :]

    h = x.reshape(T, D_MODEL)
    for l in range(depth):
        w_uk, w_uv = _pack_w_ukv(w_ukv[l])
        q, k, vt, z, xbc, dt = _proj_call(h, pos_b, invf, row(pre_mix_norm[l]), _pack_w_in(w_in[l]),
                                         row(q_norm[l]), _pack_w_uq(w_uq[l]), row(kv_norm[l]), w_uk, w_uv)
        y_att = _attn_call(q, k, vt, batch, seq)
        dt_bias_lanes = jnp.zeros((1, LANES), F32).at[0, DT_LANE0:DT_LANE0 + SSD_HEADS].set(
            dt_bias[l].astype(F32))
        y_ssd = _ssd_call(z, xbc, dt, conv_w[l].astype(F32), row(conv_b[l]), dt_bias_lanes,
                          _per_lane(a_log[l]), _per_lane(d_skip[l]), row(ssd_norm[l]), batch, seq)
        h = _mlp_call(h, y_att, y_ssd, w_out[l].astype(BF16), row(post_mix_norm[l]), row(pre_mlp_norm[l]),
                      w_up[l].astype(BF16), w_down[l].astype(BF16), row(post_mlp_norm[l]))
    return h.reshape(batch, seq, D_MODEL)
```

```python
import functools
import math

import jax
import jax.numpy as jnp
from jax import lax
from jax.experimental import pallas as pl
from jax.experimental.pallas import tpu as pltpu

F32 = jnp.float32
BF16 = jnp.bfloat16

D_MODEL = 1024
CHUNK = 64
EPS = 1e-6
MLA_HEADS = 8
QK_NOPE_DIM = 64
QK_ROPE_DIM = 32
V_HEAD_DIM = 64
Q_LORA_RANK = 768
KV_LORA_RANK = 256
ROPE_THETA = 10000.0
MLA_WIDTH = MLA_HEADS * V_HEAD_DIM
SSD_HEADS = 8
SSD_HEAD_DIM = 64
SSD_INNER = SSD_HEADS * SSD_HEAD_DIM
SSD_GROUPS = 2
SSD_STATE = 128
CONV_WIDTH = 4
CONV_DIM = SSD_INNER + 2 * SSD_GROUPS * SSD_STATE
D_FF = 4 * D_MODEL

LANES = 128
SUBLANES = 8
HEAD_PAD = LANES
HALF_ROPE = QK_ROPE_DIM // 2
ROPE_LO = QK_NOPE_DIM
ROPE_MID = QK_NOPE_DIM + HALF_ROPE
ROPE_HI = QK_NOPE_DIM + QK_ROPE_DIM
HEADS_PER_GROUP = SSD_HEADS // SSD_GROUPS
GROUP_WIDTH = HEADS_PER_GROUP * SSD_HEAD_DIM

C_Q0, C_Q1 = 0, Q_LORA_RANK
C_KV0, C_KV1 = C_Q1, C_Q1 + KV_LORA_RANK
C_KR0, C_KR1 = C_KV1, C_KV1 + HEAD_PAD
C_Z0, C_Z1 = C_KR1, C_KR1 + SSD_INNER
C_X0, C_X1 = C_Z1, C_Z1 + CONV_DIM
PROJ_COLS = C_X1
DT_LANE0 = ROPE_HI

BF16_SUBLANES = 2 * SUBLANES
VT_ROWS = V_HEAD_DIM + BF16_SUBLANES

ATTN_ROWS = 512
ATTN_HEADS = 4
PROJ_ROWS = ATTN_ROWS
SSD_ROWS = 512
MLP_ROWS = 512
FF_BLOCK = 1024
VMEM_LIMIT = 56 * 1024 * 1024


def _rms(x, w):
    return x * lax.rsqrt(jnp.mean(x * x, axis=-1, keepdims=True) + EPS) * w


def _silu(x):
    half = 0.5 * x
    return half + half * jnp.tanh(half)


def _split3(x):
    hi = x.astype(BF16)
    rest = x - hi.astype(F32)
    mid = rest.astype(BF16)
    return hi, mid, (rest - mid.astype(F32)).astype(BF16)


def _const_spec(shape):
    zeros = (0,) * len(shape)
    return pl.BlockSpec(shape, lambda *_: zeros, pipeline_mode=pl.Buffered(1))


def _proj_body(h_ref, pos_ref, invf_ref, prew_ref, w1_ref, qnw_ref, wuq_ref, kvnw_ref, wuk_ref, wuv_ref,
               q_ref, k_ref, vt_ref, z_ref, xbc_ref, dt_ref):
    u = _rms(h_ref[...], prew_ref[...]).astype(BF16)

    def seg(c0, c1):
        return jnp.dot(u, w1_ref[:, c0:c1], preferred_element_type=F32)

    c_q = seg(C_Q0, C_Q1)
    c_kv = seg(C_KV0, C_KV1)
    kr_dt = seg(C_KR0, C_KR1)
    z_ref[...] = seg(C_Z0, C_Z1)
    xbc_ref[...] = seg(C_X0, C_X1)
    dt_ref[...] = kr_dt

    lane = lax.broadcasted_iota(jnp.int32, (1, LANES), 1)
    k_rope = jnp.where(lane < ROPE_HI, kr_dt, 0.0)
    first_half = (lane >= ROPE_LO) & (lane < ROPE_MID)
    second_half = (lane >= ROPE_MID) & (lane < ROPE_HI)
    ang = pos_ref[...].astype(F32) * invf_ref[...]
    cos = jnp.where(first_half | second_half, jnp.cos(ang), 1.0)
    sin = jnp.sin(ang)
    sin_first = jnp.where(first_half, -sin, 0.0)
    sin_second = jnp.where(second_half, sin, 0.0)

    def rope(t):
        return (t * cos + pltpu.roll(t, LANES - HALF_ROPE, 1) * sin_first
                + pltpu.roll(t, HALF_ROPE, 1) * sin_second)

    scale = (QK_NOPE_DIM + QK_ROPE_DIM) ** -0.5 * math.log2(math.e)
    qf = jnp.dot(_rms(c_q, qnw_ref[...]).astype(BF16), wuq_ref[...], preferred_element_type=F32)
    c_kv_n = _rms(c_kv, kvnw_ref[...]).astype(BF16)
    kf = jnp.dot(c_kv_n, wuk_ref[...], preferred_element_type=F32)
    k_rope = rope(k_rope)
    for h in range(MLA_HEADS):
        blk = slice(h * HEAD_PAD, (h + 1) * HEAD_PAD)
        q_ref[h] = (rope(qf[:, blk]) * scale).astype(BF16)
        k_ref[h] = (kf[:, blk] + k_rope).astype(BF16)
    v_t = jnp.dot(c_kv_n, wuv_ref[...], preferred_element_type=F32).T
    ones = jnp.ones((BF16_SUBLANES, v_t.shape[1]), BF16)
    for h in range(MLA_HEADS):
        vt_ref[h, 0, 0:V_HEAD_DIM, :] = v_t[h * V_HEAD_DIM:(h + 1) * V_HEAD_DIM, :].astype(BF16)
        vt_ref[h, 0, V_HEAD_DIM:VT_ROWS, :] = ones


def _proj_call(h, pos_b, invf, prew, w1, qnw, wuq, kvnw, wuk, wuv):
    T = h.shape[0]
    rows = PROJ_ROWS
    row_spec = lambda cols: pl.BlockSpec((rows, cols), lambda i: (i, 0))
    head_spec = pl.BlockSpec((MLA_HEADS, rows, HEAD_PAD), lambda i: (0, i, 0))
    return pl.pallas_call(
        _proj_body,
        grid=(T // rows,),
        in_specs=[row_spec(D_MODEL), row_spec(LANES), _const_spec((1, LANES)), _const_spec((1, D_MODEL)),
                  _const_spec((D_MODEL, PROJ_COLS)), _const_spec((1, Q_LORA_RANK)),
                  _const_spec((Q_LORA_RANK, MLA_HEADS * HEAD_PAD)), _const_spec((1, KV_LORA_RANK)),
                  _const_spec((KV_LORA_RANK, MLA_HEADS * HEAD_PAD)), _const_spec((KV_LORA_RANK, MLA_WIDTH))],
        out_specs=[head_spec, head_spec,
                   pl.BlockSpec((MLA_HEADS, 1, VT_ROWS, rows), lambda i: (0, i, 0, 0)),
                   row_spec(SSD_INNER), row_spec(CONV_DIM), row_spec(LANES)],
        out_shape=[jax.ShapeDtypeStruct((MLA_HEADS, T, HEAD_PAD), BF16),
                   jax.ShapeDtypeStruct((MLA_HEADS, T, HEAD_PAD), BF16),
                   jax.ShapeDtypeStruct((MLA_HEADS, T // rows, VT_ROWS, rows), BF16),
                   jax.ShapeDtypeStruct((T, SSD_INNER), F32),
                   jax.ShapeDtypeStruct((T, CONV_DIM), F32),
                   jax.ShapeDtypeStruct((T, LANES), F32)],
        compiler_params=pltpu.CompilerParams(dimension_semantics=("parallel",), vmem_limit_bytes=VMEM_LIMIT),
        name="proj",
    )(h, pos_b, invf, prew, w1, qnw, wuq, kvnw, wuk, wuv)


def _attn_body(q_ref, k_ref, vt_ref, bias_ref, o_ref, s0_scr, s1_scr, tmax_scr, m_scr, acc_scr, *, rows):
    qi = pl.program_id(2)
    s_scr = (s0_scr, s1_scr)

    def scores(hh, j, slot):
        start = pl.multiple_of(j * rows, rows)
        k = k_ref[hh, pl.ds(start, rows), :]
        s_t = lax.dot_general(k, q_ref[hh], (((1,), (1,)), ((), ())), preferred_element_type=F32)
        s_scr[slot][hh] = s_t
        tmax_scr[slot, hh] = jnp.max(s_t, axis=0, keepdims=True)

    def update(hh, j, slot, diagonal=False):
        s_t = s_scr[slot][hh]
        if diagonal:
            s_t = s_t + bias_ref[...]
            tile_max = jnp.max(s_t, axis=0, keepdims=True)
        else:
            tile_max = tmax_scr[slot, hh]
        m_prev = m_scr[hh]
        m_new = jnp.maximum(m_prev, tile_max)
        p_t = jnp.exp2(s_t - m_new).astype(BF16)
        contrib = jnp.dot(vt_ref[hh, j], p_t, preferred_element_type=F32)
        acc_scr[hh] = acc_scr[hh] * jnp.exp2(m_prev - m_new) + contrib
        m_scr[hh] = m_new

    m_scr[...] = jnp.full_like(m_scr, -jnp.inf)
    acc_scr[...] = jnp.zeros_like(acc_scr)
    for hh in range(ATTN_HEADS):
        scores(hh, 0, 0)

    def step(j, slot, diagonal=False):
        for hh in range(ATTN_HEADS):
            scores(hh, j + 1, 1 - slot)
            update(hh, j, slot, diagonal)

    def tile_pair(i, carry):
        step(2 * i, 0)
        step(2 * i + 1, 1)
        return carry

    lax.fori_loop(0, qi // 2, tile_pair, 0)

    @pl.when(qi % 2 == 0)
    def _():
        for hh in range(ATTN_HEADS):
            update(hh, qi, 0, diagonal=True)

    @pl.when(qi % 2 == 1)
    def _():
        step(qi - 1, 0)
        for hh in range(ATTN_HEADS):
            update(hh, qi, 1, diagonal=True)

    out_t = [acc_scr[hh, 0:V_HEAD_DIM, :] / acc_scr[hh, V_HEAD_DIM:V_HEAD_DIM + 1, :]
             for hh in range(ATTN_HEADS)]
    o_ref[...] = jnp.concatenate(out_t, axis=0).T.astype(o_ref.dtype)


def _attn_call(q, k, vt, batch, seq):
    rows = ATTN_ROWS
    n_q = seq // rows
    groups = MLA_HEADS // ATTN_HEADS
    chunk_of = jnp.arange(rows, dtype=jnp.int32) // CHUNK
    bias = jnp.where(chunk_of[:, None] <= chunk_of[None, :], 0.0, -jnp.inf).astype(F32)
    return pl.pallas_call(
        functools.partial(_attn_body, rows=rows),
        grid=(batch, groups, n_q),
        in_specs=[pl.BlockSpec((ATTN_HEADS, rows, HEAD_PAD), lambda b, g, i: (g, b * n_q + i, 0)),
                  pl.BlockSpec((ATTN_HEADS, seq, HEAD_PAD), lambda b, g, i: (g, b, 0)),
                  pl.BlockSpec((ATTN_HEADS, n_q, VT_ROWS, rows), lambda b, g, i: (g, b, 0, 0)),
                  _const_spec((rows, rows))],
        out_specs=pl.BlockSpec((rows, ATTN_HEADS * V_HEAD_DIM), lambda b, g, i: (b * n_q + i, g)),
        out_shape=jax.ShapeDtypeStruct((batch * seq, MLA_WIDTH), BF16),
        scratch_shapes=[pltpu.VMEM((ATTN_HEADS, rows, rows), F32), pltpu.VMEM((ATTN_HEADS, rows, rows), F32),
                        pltpu.VMEM((2, ATTN_HEADS, 1, rows), F32), pltpu.VMEM((ATTN_HEADS, 1, rows), F32),
                        pltpu.VMEM((ATTN_HEADS, VT_ROWS, rows), F32)],
        compiler_params=pltpu.CompilerParams(dimension_semantics=("parallel", "parallel", "arbitrary"),
                                             vmem_limit_bytes=VMEM_LIMIT),
        name="attn",
    )(q, k, vt, bias)


def _ssd_body(z_ref, xbc_ref, dt_ref, convw_ref, convb_ref, dtb_ref, expand_ref, alog_ref, dskip_ref, normw_ref,
              o_ref, ext_scr, act_scr, dts_scr, y_scr, state_scr, *, rows):
    tail = SUBLANES
    L = CHUNK

    @pl.when(pl.program_id(1) == 0)
    def _():
        ext_scr[0:tail, :] = jnp.zeros((tail, CONV_DIM), F32)
        state_scr[...] = jnp.zeros_like(state_scr)

    ext_scr[tail:tail + rows, :] = xbc_ref[...]
    ext = ext_scr[...]
    conv = ext[tail:tail + rows, :] * convw_ref[CONV_WIDTH - 1:CONV_WIDTH, :]
    for back in range(1, CONV_WIDTH):
        w_row = convw_ref[CONV_WIDTH - 1 - back:CONV_WIDTH - back, :]
        conv = conv + pltpu.roll(ext, back, 0)[tail:tail + rows, :] * w_row
    conv = conv + convb_ref[...]
    act_scr[...] = _silu(conv)
    ext_scr[0:tail, :] = ext_scr[rows:rows + tail, :]
    dt_heads = jax.nn.softplus(dt_ref[...] + dtb_ref[...])
    dts_scr[...] = jnp.dot(jnp.concatenate(_split3(dt_heads), axis=1), expand_ref[...],
                           preferred_element_type=F32)

    a_coef = -jnp.exp(alog_ref[...]) * math.log2(math.e)
    row_i = lax.broadcasted_iota(jnp.int32, (L, SSD_INNER), 0)
    col_j = lax.broadcasted_iota(jnp.int32, (L, SSD_INNER), 1) % SSD_HEAD_DIM
    upper = (row_i <= col_j).astype(F32)
    lower = col_j <= row_i
    tril = (lax.broadcasted_iota(jnp.int32, (L, L), 1) <= lax.broadcasted_iota(jnp.int32, (L, L), 0)).astype(BF16)
    tril3 = jnp.concatenate([tril] * 3, axis=1)
    blk_r = lax.broadcasted_iota(jnp.int32, (GROUP_WIDTH, GROUP_WIDTH), 0) // SSD_HEAD_DIM
    blk_c = lax.broadcasted_iota(jnp.int32, (GROUP_WIDTH, GROUP_WIDTH), 1) // SSD_HEAD_DIM
    same_head = blk_r == blk_c

    def chunk(c, carry):
        r0 = c * L
        xs = act_scr[pl.ds(r0, L), 0:SSD_INNER]
        dt = dts_scr[pl.ds(r0, L), :]
        a = dt * a_coef
        xdt = xs * dt
        a_cs = jnp.dot(tril3, jnp.concatenate(_split3(a), axis=0), preferred_element_type=F32)
        a_cs_row = jnp.sum(a * upper, axis=0, keepdims=True)
        decay_in = jnp.exp2(jnp.where(lower, a_cs - a_cs_row, -jnp.inf))
        a_last = a_cs[L - 1:L, :]
        decay_to_end = jnp.exp2(a_last - a_cs)
        decay_from_start = jnp.exp2(a_cs)
        chunk_decay = jnp.exp2(a_last)
        for g in range(SSD_GROUPS):
            gl = slice(g * GROUP_WIDTH, (g + 1) * GROUP_WIDTH)
            b_off = SSD_INNER + g * SSD_STATE
            c_off = SSD_INNER + SSD_GROUPS * SSD_STATE + g * SSD_STATE
            b_f32 = act_scr[pl.ds(r0, L), b_off:b_off + SSD_STATE]
            b_g = b_f32.astype(BF16)
            c_g = act_scr[pl.ds(r0, L), c_off:c_off + SSD_STATE].astype(BF16)
            b_rep = jnp.concatenate([b_g] * HEADS_PER_GROUP, axis=0)
            cb = lax.dot_general(c_g, b_rep, (((1,), (1,)), ((), ())), preferred_element_type=F32)
            scores = (cb * decay_in[:, gl]).astype(BF16)
            x_g = xdt[:, gl]
            x_rep = jnp.concatenate([x_g] * HEADS_PER_GROUP, axis=0)
            x_diag = jnp.where(same_head, x_rep, 0.0).astype(BF16)
            y_diag = jnp.dot(scores, x_diag, preferred_element_type=F32)
            state = state_scr[g]
            y_off = jnp.dot(c_g, state.astype(BF16), preferred_element_type=F32) * decay_from_start[:, gl]
            x_dec = (x_g * decay_to_end[:, gl]).astype(BF16)
            new_state = jnp.dot(b_f32.T.astype(BF16), x_dec, preferred_element_type=F32)
            state_scr[g] = state * chunk_decay[:, gl] + new_state
            y_scr[pl.ds(r0, L), gl] = y_diag + y_off + dskip_ref[:, gl] * xs[:, gl]
        return carry

    for c in range(rows // L):
        chunk(c, 0)

    zf = z_ref[...]
    y = y_scr[...] * _silu(zf)
    for g in range(SSD_GROUPS):
        gl = slice(g * GROUP_WIDTH, (g + 1) * GROUP_WIDTH)
        yg = y[:, gl]
        yg = yg * lax.rsqrt(jnp.mean(yg * yg, axis=-1, keepdims=True) + EPS)
        o_ref[:, gl] = (yg * normw_ref[:, gl]).astype(o_ref.dtype)


def _ssd_call(z, xbc, dt, convw, convb, dtb, alog, dskip, normw, batch, seq):
    rows = SSD_ROWS
    n_t = seq // rows
    row_spec = lambda cols: pl.BlockSpec((rows, cols), lambda b, i: (b * n_t + i, 0))
    src_head = jnp.arange(LANES, dtype=jnp.int32)[:, None] - DT_LANE0
    dst_head = jnp.arange(SSD_INNER, dtype=jnp.int32)[None, :] // SSD_HEAD_DIM
    expand = jnp.tile((src_head == dst_head).astype(BF16), (3, 1))
    return pl.pallas_call(
        functools.partial(_ssd_body, rows=rows),
        grid=(batch, n_t),
        in_specs=[row_spec(SSD_INNER), row_spec(CONV_DIM), row_spec(LANES),
                  _const_spec((CONV_WIDTH, CONV_DIM)), _const_spec((1, CONV_DIM)), _const_spec((1, LANES)),
                  _const_spec((3 * LANES, SSD_INNER)),
                  _const_spec((1, SSD_INNER)), _const_spec((1, SSD_INNER)), _const_spec((1, SSD_INNER))],
        out_specs=row_spec(SSD_INNER),
        out_shape=jax.ShapeDtypeStruct((batch * seq, SSD_INNER), BF16),
        scratch_shapes=[pltpu.VMEM((rows + SUBLANES, CONV_DIM), F32), pltpu.VMEM((rows, CONV_DIM), F32),
                        pltpu.VMEM((rows, SSD_INNER), F32), pltpu.VMEM((rows, SSD_INNER), F32),
                        pltpu.VMEM((SSD_GROUPS, SSD_STATE, GROUP_WIDTH), F32)],
        compiler_params=pltpu.CompilerParams(dimension_semantics=("arbitrary", "arbitrary"),
                                             vmem_limit_bytes=VMEM_LIMIT),
        name="ssd",
    )(z, xbc, dt, convw, convb, dtb, expand, alog, dskip, normw)


def _mlp_body(h_ref, ya_ref, ys_ref, wout_ref, postmix_ref, premlp_ref, wup_ref, wdn_ref, postmlp_ref, o_ref):
    mixed = (jnp.dot(ya_ref[...], wout_ref[0:MLA_WIDTH, :], preferred_element_type=F32)
             + jnp.dot(ys_ref[...], wout_ref[MLA_WIDTH:, :], preferred_element_type=F32))
    h1 = h_ref[...] + _rms(mixed, postmix_ref[...])
    m = _rms(h1, premlp_ref[...]).astype(BF16)
    acc = None
    for c in range(D_FF // FF_BLOCK):
        ff = slice(c * FF_BLOCK, (c + 1) * FF_BLOCK)
        up = jnp.dot(m, wup_ref[:, ff], preferred_element_type=F32)
        act = jnp.square(jnp.maximum(up, 0.0)).astype(BF16)
        part = jnp.dot(act, wdn_ref[ff, :], preferred_element_type=F32)
        acc = part if acc is None else acc + part
    o_ref[...] = h1 + _rms(acc, postmlp_ref[...])


def _mlp_call(h, ya, ys, wout, postmix, premlp, wup, wdn, postmlp):
    T = h.shape[0]
    rows = MLP_ROWS
    row_spec = lambda cols: pl.BlockSpec((rows, cols), lambda i: (i, 0))
    return pl.pallas_call(
        _mlp_body,
        grid=(T // rows,),
        in_specs=[row_spec(D_MODEL), row_spec(MLA_WIDTH), row_spec(SSD_INNER),
                  _const_spec((MLA_WIDTH + SSD_INNER, D_MODEL)), _const_spec((1, D_MODEL)),
                  _const_spec((1, D_MODEL)), _const_spec((D_MODEL, D_FF)), _const_spec((D_FF, D_MODEL)),
                  _const_spec((1, D_MODEL))],
        out_specs=row_spec(D_MODEL),
        out_shape=jax.ShapeDtypeStruct((T, D_MODEL), F32),
        compiler_params=pltpu.CompilerParams(dimension_semantics=("parallel",), vmem_limit_bytes=VMEM_LIMIT),
        name="mlp",
    )(h, ya, ys, wout, postmix, premlp, wup, wdn, postmlp)


def _pack_w_in(w_in):
    s1 = Q_LORA_RANK
    s2 = s1 + KV_LORA_RANK
    s3 = s2 + QK_ROPE_DIM
    s4 = s3 + SSD_INNER
    s5 = s4 + CONV_DIM
    zeros = lambda n: jnp.zeros((D_MODEL, n), w_in.dtype)
    kr_dt = jnp.concatenate([zeros(ROPE_LO), w_in[:, s2:s3], w_in[:, s5:],
                             zeros(HEAD_PAD - DT_LANE0 - SSD_HEADS)], axis=1)
    return jnp.concatenate([w_in[:, :s2], kr_dt, w_in[:, s3:s5]], axis=1).astype(BF16)


def _pack_w_uq(w_uq):
    w = w_uq.reshape(Q_LORA_RANK, MLA_HEADS, QK_NOPE_DIM + QK_ROPE_DIM)
    pad = jnp.zeros((Q_LORA_RANK, MLA_HEADS, HEAD_PAD - ROPE_HI), w_uq.dtype)
    return jnp.concatenate([w, pad], axis=-1).reshape(Q_LORA_RANK, MLA_HEADS * HEAD_PAD).astype(BF16)


def _pack_w_ukv(w_ukv):
    w = w_ukv.reshape(KV_LORA_RANK, MLA_HEADS, QK_NOPE_DIM + V_HEAD_DIM)
    pad = jnp.zeros((KV_LORA_RANK, MLA_HEADS, HEAD_PAD - QK_NOPE_DIM), w_ukv.dtype)
    w_uk = jnp.concatenate([w[..., :QK_NOPE_DIM], pad], axis=-1).reshape(KV_LORA_RANK, MLA_HEADS * HEAD_PAD)
    w_uv = w[..., QK_NOPE_DIM:].reshape(KV_LORA_RANK, MLA_WIDTH)
    return w_uk.astype(BF16), w_uv.astype(BF16)


def _per_lane(v):
    return jnp.repeat(v.astype(F32), SSD_HEAD_DIM)[None, :]


def kernel(x, positions, pre_mix_norm, w_in, q_norm, w_uq, kv_norm, w_ukv, conv_w, conv_b, dt_bias, a_log,
           d_skip, ssd_norm, w_out, post_mix_norm, pre_mlp_norm, w_up, w_down, post_mlp_norm):
    batch, seq, _ = x.shape
    depth = w_in.shape[0]
    T = batch * seq
    assert seq % ATTN_ROWS == 0 and seq % SSD_ROWS == 0 and T % PROJ_ROWS == 0 and T % MLP_ROWS == 0

    inv_freq = ROPE_THETA ** (-jnp.arange(0, QK_ROPE_DIM, 2, dtype=F32) / QK_ROPE_DIM)
    invf = jnp.concatenate([jnp.zeros((ROPE_LO,), F32), inv_freq, inv_freq,
                            jnp.zeros((HEAD_PAD - ROPE_HI,), F32)])[None, :]
    pos_b = jnp.broadcast_to(positions.reshape(T, 1), (T, LANES))
    row = lambda v: v.astype(F32)[None, :]

    h = x.reshape(T, D_MODEL)
    for l in range(depth):
        w_uk, w_uv = _pack_w_ukv(w_ukv[l])
        q, k, vt, z, xbc, dt = _proj_call(h, pos_b, invf, row(pre_mix_norm[l]), _pack_w_in(w_in[l]),
                                         row(q_norm[l]), _pack_w_uq(w_uq[l]), row(kv_norm[l]), w_uk, w_uv)
        y_att = _attn_call(q, k, vt, batch, seq)
        dt_bias_lanes = jnp.zeros((1, LANES), F32).at[0, DT_LANE0:DT_LANE0 + SSD_HEADS].set(
            dt_bias[l].astype(F32))
        y_ssd = _ssd_call(z, xbc, dt, conv_w[l].astype(F32), row(conv_b[l]), dt_bias_lanes,
                          _per_lane(a_log[l]), _per_lane(d_skip[l]), row(ssd_norm[l]), batch, seq)
        h = _mlp_call(h, y_att, y_ssd, w_out[l].astype(BF16), row(post_mix_norm[l]), row(pre_mlp_norm[l]),
                      w_up[l].astype(BF16), w_down[l].astype(BF16), row(post_mlp_norm[l]))
    return h.reshape(batch, seq, D_MODEL)
```

```python
import functools
import math

import jax
import jax.numpy as jnp
from jax import lax
from jax.experimental import pallas as pl
from jax.experimental.pallas import tpu as pltpu

F32 = jnp.float32
BF16 = jnp.bfloat16

D_MODEL = 1024
CHUNK = 64
EPS = 1e-6
MLA_HEADS = 8
QK_NOPE_DIM = 64
QK_ROPE_DIM = 32
V_HEAD_DIM = 64
Q_LORA_RANK = 768
KV_LORA_RANK = 256
ROPE_THETA = 10000.0
MLA_WIDTH = MLA_HEADS * V_HEAD_DIM
SSD_HEADS = 8
SSD_HEAD_DIM = 64
SSD_INNER = SSD_HEADS * SSD_HEAD_DIM
SSD_GROUPS = 2
SSD_STATE = 128
CONV_WIDTH = 4
CONV_DIM = SSD_INNER + 2 * SSD_GROUPS * SSD_STATE
D_FF = 4 * D_MODEL

LANES = 128
SUBLANES = 8
HEAD_PAD = LANES
HALF_ROPE = QK_ROPE_DIM // 2
ROPE_LO = QK_NOPE_DIM
ROPE_MID = QK_NOPE_DIM + HALF_ROPE
ROPE_HI = QK_NOPE_DIM + QK_ROPE_DIM
HEADS_PER_GROUP = SSD_HEADS // SSD_GROUPS
GROUP_WIDTH = HEADS_PER_GROUP * SSD_HEAD_DIM

C_Q0, C_Q1 = 0, Q_LORA_RANK
C_KV0, C_KV1 = C_Q1, C_Q1 + KV_LORA_RANK
C_KR0, C_KR1 = C_KV1, C_KV1 + HEAD_PAD
C_Z0, C_Z1 = C_KR1, C_KR1 + SSD_INNER
C_X0, C_X1 = C_Z1, C_Z1 + CONV_DIM
PROJ_COLS = C_X1
DT_LANE0 = ROPE_HI

BF16_SUBLANES = 2 * SUBLANES
VT_ROWS = V_HEAD_DIM + BF16_SUBLANES

ATTN_ROWS = 512
ATTN_HEADS = 4
PROJ_ROWS = ATTN_ROWS
PROJ_SLAB = 2 * LANES
MLP_ROWS = 512
FF_BLOCK = 1024
VMEM_LIMIT = 56 * 1024 * 1024


def _rms(x, w):
    return x * lax.rsqrt(jnp.mean(x * x, axis=-1, keepdims=True) + EPS) * w


def _silu(x):
    half = 0.5 * x
    return half + half * jnp.tanh(half)


def _split3(x):
    hi = x.astype(BF16)
    rest = x - hi.astype(F32)
    mid = rest.astype(BF16)
    return hi, mid, (rest - mid.astype(F32)).astype(BF16)


def _const_spec(shape):
    zeros = (0,) * len(shape)
    return pl.BlockSpec(shape, lambda *_: zeros, pipeline_mode=pl.Buffered(1))


def _proj_body(h_ref, pos_ref, invf_ref, prew_ref, w1_ref, qnw_ref, wuq_ref, kvnw_ref, wuk_ref, wuv_ref,
               convw_ref, convb_ref, dtb_ref, expand_ref, alog_ref, dskip_ref, normw_ref,
               q_ref, k_ref, vt_ref, y_ref, ext_scr, act_scr, z_scr, dtr_scr, dts_scr, y_scr, state_scr,
               *, rows):
    i = pl.program_id(1)

    @pl.when((pl.program_id(0) == 0) & (i == 0))
    def _():
        act_scr[...] = jnp.zeros_like(act_scr)
        z_scr[...] = jnp.zeros_like(z_scr)
        dtr_scr[...] = jnp.zeros_like(dtr_scr)
        state_scr[...] = jnp.zeros_like(state_scr)

    @pl.when(i == 0)
    def _():
        ext_scr[0:SUBLANES, :] = jnp.zeros((SUBLANES, CONV_DIM), F32)

    @pl.when(i == 1)
    def _():
        state_scr[...] = jnp.zeros_like(state_scr)

    scan = _ssd_scan_stages(z_scr, dtr_scr, dtb_ref, expand_ref, alog_ref, dskip_ref, normw_ref, y_ref,
                            act_scr, dts_scr, y_scr, state_scr, rows)
    u = _rms(h_ref[...], prew_ref[...]).astype(BF16)

    slabs = [(c0, min(c0 + PROJ_SLAB, PROJ_COLS)) for c0 in range(0, PROJ_COLS, PROJ_SLAB)]
    scan[0]()
    middle = scan[1:-1]
    pieces, done = [], 0
    for n, (c0, c1) in enumerate(slabs):
        pieces.append(jnp.dot(u, w1_ref[:, c0:c1], preferred_element_type=F32))
        upto = (len(middle) * (n + 1)) // (len(slabs) - 1)
        for stage in middle[done:upto]:
            stage()
        done = max(done, min(upto, len(middle)))
    proj = jnp.concatenate(pieces, axis=1)
    scan[-1]()

    c_q = proj[:, C_Q0:C_Q1]
    c_kv = proj[:, C_KV0:C_KV1]
    kr_dt = proj[:, C_KR0:C_KR1]
    z_scr[...] = proj[:, C_Z0:C_Z1]
    dtr_scr[...] = kr_dt
    _ssd_conv(proj[:, C_X0:C_X1], convw_ref, convb_ref, ext_scr, act_scr, rows)

    lane = lax.broadcasted_iota(jnp.int32, (1, LANES), 1)
    k_rope = jnp.where(lane < ROPE_HI, kr_dt, 0.0)
    first_half = (lane >= ROPE_LO) & (lane < ROPE_MID)
    second_half = (lane >= ROPE_MID) & (lane < ROPE_HI)
    ang = pos_ref[...].astype(F32) * invf_ref[...]
    cos = jnp.where(first_half | second_half, jnp.cos(ang), 1.0)
    sin = jnp.sin(ang)
    sin_first = jnp.where(first_half, -sin, 0.0)
    sin_second = jnp.where(second_half, sin, 0.0)

    def rope(t):
        return (t * cos + pltpu.roll(t, LANES - HALF_ROPE, 1) * sin_first
                + pltpu.roll(t, HALF_ROPE, 1) * sin_second)

    scale = (QK_NOPE_DIM + QK_ROPE_DIM) ** -0.5 * math.log2(math.e)
    qf = jnp.dot(_rms(c_q, qnw_ref[...]).astype(BF16), wuq_ref[...], preferred_element_type=F32)
    c_kv_n = _rms(c_kv, kvnw_ref[...]).astype(BF16)
    kf = jnp.dot(c_kv_n, wuk_ref[...], preferred_element_type=F32)
    k_rope = rope(k_rope)
    for h in range(MLA_HEADS):
        blk = slice(h * HEAD_PAD, (h + 1) * HEAD_PAD)
        q_ref[h] = (rope(qf[:, blk]) * scale).astype(BF16)
        k_ref[h] = (kf[:, blk] + k_rope).astype(BF16)
    v_t = jnp.dot(c_kv_n, wuv_ref[...], preferred_element_type=F32).T
    ones = jnp.ones((BF16_SUBLANES, v_t.shape[1]), BF16)
    for h in range(MLA_HEADS):
        vt_ref[h, 0, 0:V_HEAD_DIM, :] = v_t[h * V_HEAD_DIM:(h + 1) * V_HEAD_DIM, :].astype(BF16)
        vt_ref[h, 0, V_HEAD_DIM:VT_ROWS, :] = ones


def _proj_call(h, pos_b, invf, prew, w1, qnw, wuq, kvnw, wuk, wuv, convw, convb, dtb, alog, dskip, normw,
               batch, seq):
    T = batch * seq
    rows = PROJ_ROWS
    n_t = seq // rows
    tile = lambda b, i: b * n_t + jnp.minimum(i, n_t - 1)
    row_spec = lambda cols: pl.BlockSpec((rows, cols), lambda b, i: (tile(b, i), 0))
    head_spec = pl.BlockSpec((MLA_HEADS, rows, HEAD_PAD), lambda b, i: (0, tile(b, i), 0))
    scan_spec = pl.BlockSpec((rows, SSD_INNER), lambda b, i: (b * n_t + jnp.maximum(i - 1, 0), 0))
    src_head = jnp.arange(LANES, dtype=jnp.int32)[:, None] - DT_LANE0
    dst_head = jnp.arange(SSD_INNER, dtype=jnp.int32)[None, :] // SSD_HEAD_DIM
    expand = jnp.tile((src_head == dst_head).astype(BF16), (3, 1))
    return pl.pallas_call(
        functools.partial(_proj_body, rows=rows),
        grid=(batch, n_t + 1),
        in_specs=[row_spec(D_MODEL), row_spec(LANES), _const_spec((1, LANES)), _const_spec((1, D_MODEL)),
                  _const_spec((D_MODEL, PROJ_COLS)), _const_spec((1, Q_LORA_RANK)),
                  _const_spec((Q_LORA_RANK, MLA_HEADS * HEAD_PAD)), _const_spec((1, KV_LORA_RANK)),
                  _const_spec((KV_LORA_RANK, MLA_HEADS * HEAD_PAD)), _const_spec((KV_LORA_RANK, MLA_WIDTH)),
                  _const_spec((CONV_WIDTH, CONV_DIM)), _const_spec((1, CONV_DIM)), _const_spec((1, LANES)),
                  _const_spec((3 * LANES, SSD_INNER)),
                  _const_spec((1, SSD_INNER)), _const_spec((1, SSD_INNER)), _const_spec((1, SSD_INNER))],
        out_specs=[head_spec, head_spec,
                   pl.BlockSpec((MLA_HEADS, 1, VT_ROWS, rows), lambda b, i: (0, tile(b, i), 0, 0)),
                   scan_spec],
        out_shape=[jax.ShapeDtypeStruct((MLA_HEADS, T, HEAD_PAD), BF16),
                   jax.ShapeDtypeStruct((MLA_HEADS, T, HEAD_PAD), BF16),
                   jax.ShapeDtypeStruct((MLA_HEADS, T // rows, VT_ROWS, rows), BF16),
                   jax.ShapeDtypeStruct((T, SSD_INNER), BF16)],
        scratch_shapes=[pltpu.VMEM((rows + SUBLANES, CONV_DIM), F32), pltpu.VMEM((rows, CONV_DIM), F32),
                        pltpu.VMEM((rows, SSD_INNER), F32), pltpu.VMEM((rows, LANES), F32),
                        pltpu.VMEM((rows, SSD_INNER), F32), pltpu.VMEM((rows, SSD_INNER), F32),
                        pltpu.VMEM((SSD_GROUPS, SSD_STATE, GROUP_WIDTH), F32)],
        compiler_params=pltpu.CompilerParams(dimension_semantics=("arbitrary", "arbitrary"),
                                             vmem_limit_bytes=VMEM_LIMIT),
        name="proj_ssd",
    )(h, pos_b, invf, prew, w1, qnw, wuq, kvnw, wuk, wuv, convw, convb, dtb, expand, alog, dskip, normw)


def _attn_body(q_ref, k_ref, vt_ref, bias_ref, o_ref, s0_scr, s1_scr, tmax_scr, m_scr, acc_scr, *, rows):
    qi = pl.program_id(2)
    s_scr = (s0_scr, s1_scr)

    def scores(hh, j, slot):
        start = pl.multiple_of(j * rows, rows)
        k = k_ref[hh, pl.ds(start, rows), :]
        s_t = lax.dot_general(k, q_ref[hh], (((1,), (1,)), ((), ())), preferred_element_type=F32)
        s_scr[slot][hh] = s_t
        tmax_scr[slot, hh] = jnp.max(s_t, axis=0, keepdims=True)

    def update(hh, j, slot, diagonal=False):
        s_t = s_scr[slot][hh]
        if diagonal:
            s_t = s_t + bias_ref[...]
            tile_max = jnp.max(s_t, axis=0, keepdims=True)
        else:
            tile_max = tmax_scr[slot, hh]
        m_prev = m_scr[hh]
        m_new = jnp.maximum(m_prev, tile_max)
        p_t = jnp.exp2(s_t - m_new).astype(BF16)
        contrib = jnp.dot(vt_ref[hh, j], p_t, preferred_element_type=F32)
        acc_scr[hh] = acc_scr[hh] * jnp.exp2(m_prev - m_new) + contrib
        m_scr[hh] = m_new

    m_scr[...] = jnp.full_like(m_scr, -jnp.inf)
    acc_scr[...] = jnp.zeros_like(acc_scr)
    for hh in range(ATTN_HEADS):
        scores(hh, 0, 0)

    def step(j, slot, diagonal=False):
        for hh in range(ATTN_HEADS):
            scores(hh, j + 1, 1 - slot)
            update(hh, j, slot, diagonal)

    def tile_pair(i, carry):
        step(2 * i, 0)
        step(2 * i + 1, 1)
        return carry

    lax.fori_loop(0, qi // 2, tile_pair, 0)

    @pl.when(qi % 2 == 0)
    def _():
        for hh in range(ATTN_HEADS):
            update(hh, qi, 0, diagonal=True)

    @pl.when(qi % 2 == 1)
    def _():
        step(qi - 1, 0)
        for hh in range(ATTN_HEADS):
            update(hh, qi, 1, diagonal=True)

    out_t = [acc_scr[hh, 0:V_HEAD_DIM, :] / acc_scr[hh, V_HEAD_DIM:V_HEAD_DIM + 1, :]
             for hh in range(ATTN_HEADS)]
    o_ref[...] = jnp.concatenate(out_t, axis=0).T.astype(o_ref.dtype)


def _attn_call(q, k, vt, batch, seq):
    rows = ATTN_ROWS
    n_q = seq // rows
    groups = MLA_HEADS // ATTN_HEADS
    chunk_of = jnp.arange(rows, dtype=jnp.int32) // CHUNK
    bias = jnp.where(chunk_of[:, None] <= chunk_of[None, :], 0.0, -jnp.inf).astype(F32)
    return pl.pallas_call(
        functools.partial(_attn_body, rows=rows),
        grid=(batch, groups, n_q),
        in_specs=[pl.BlockSpec((ATTN_HEADS, rows, HEAD_PAD), lambda b, g, i: (g, b * n_q + i, 0)),
                  pl.BlockSpec((ATTN_HEADS, seq, HEAD_PAD), lambda b, g, i: (g, b, 0)),
                  pl.BlockSpec((ATTN_HEADS, n_q, VT_ROWS, rows), lambda b, g, i: (g, b, 0, 0)),
                  _const_spec((rows, rows))],
        out_specs=pl.BlockSpec((rows, ATTN_HEADS * V_HEAD_DIM), lambda b, g, i: (b * n_q + i, g)),
        out_shape=jax.ShapeDtypeStruct((batch * seq, MLA_WIDTH), BF16),
        scratch_shapes=[pltpu.VMEM((ATTN_HEADS, rows, rows), F32), pltpu.VMEM((ATTN_HEADS, rows, rows), F32),
                        pltpu.VMEM((2, ATTN_HEADS, 1, rows), F32), pltpu.VMEM((ATTN_HEADS, 1, rows), F32),
                        pltpu.VMEM((ATTN_HEADS, VT_ROWS, rows), F32)],
        compiler_params=pltpu.CompilerParams(dimension_semantics=("parallel", "parallel", "arbitrary"),
                                             vmem_limit_bytes=VMEM_LIMIT),
        name="attn",
    )(q, k, vt, bias)


def _ssd_conv(xbc, convw_ref, convb_ref, ext_scr, act_scr, rows):
    tail = SUBLANES
    ext_scr[tail:tail + rows, :] = xbc
    ext = ext_scr[...]
    conv = ext[tail:tail + rows, :] * convw_ref[CONV_WIDTH - 1:CONV_WIDTH, :]
    for back in range(1, CONV_WIDTH):
        w_row = convw_ref[CONV_WIDTH - 1 - back:CONV_WIDTH - back, :]
        conv = conv + pltpu.roll(ext, back, 0)[tail:tail + rows, :] * w_row
    conv = conv + convb_ref[...]
    act_scr[...] = _silu(conv)
    ext_scr[0:tail, :] = ext_scr[rows:rows + tail, :]


def _ssd_scan_stages(z_ref, dtr_ref, dtb_ref, expand_ref, alog_ref, dskip_ref, normw_ref, o_ref,
                     act_scr, dts_scr, y_scr, state_scr, rows):
    L = CHUNK
    shared = {}

    def prepare():
        dt_heads = jax.nn.softplus(dtr_ref[...] + dtb_ref[...])
        dts_scr[...] = jnp.dot(jnp.concatenate(_split3(dt_heads), axis=1), expand_ref[...],
                               preferred_element_type=F32)
        shared["a_coef"] = -jnp.exp(alog_ref[...]) * math.log2(math.e)
        row_i = lax.broadcasted_iota(jnp.int32, (L, SSD_INNER), 0)
        col_j = lax.broadcasted_iota(jnp.int32, (L, SSD_INNER), 1) % SSD_HEAD_DIM
        shared["upper"] = (row_i <= col_j).astype(F32)
        shared["lower"] = col_j <= row_i
        tril = (lax.broadcasted_iota(jnp.int32, (L, L), 1)
                <= lax.broadcasted_iota(jnp.int32, (L, L), 0)).astype(BF16)
        shared["tril3"] = jnp.concatenate([tril] * 3, axis=1)
        blk_r = lax.broadcasted_iota(jnp.int32, (GROUP_WIDTH, GROUP_WIDTH), 0) // SSD_HEAD_DIM
        blk_c = lax.broadcasted_iota(jnp.int32, (GROUP_WIDTH, GROUP_WIDTH), 1) // SSD_HEAD_DIM
        shared["same_head"] = blk_r == blk_c

    def first(c):
        r0 = c * L
        xs = act_scr[pl.ds(r0, L), 0:SSD_INNER]
        dt = dts_scr[pl.ds(r0, L), :]
        a = dt * shared["a_coef"]
        a_cs = jnp.dot(shared["tril3"], jnp.concatenate(_split3(a), axis=0), preferred_element_type=F32)
        a_cs_row = jnp.sum(a * shared["upper"], axis=0, keepdims=True)
        groups = []
        for g in range(SSD_GROUPS):
            b_off = SSD_INNER + g * SSD_STATE
            c_off = SSD_INNER + SSD_GROUPS * SSD_STATE + g * SSD_STATE
            b_f32 = act_scr[pl.ds(r0, L), b_off:b_off + SSD_STATE]
            b_g = b_f32.astype(BF16)
            c_g = act_scr[pl.ds(r0, L), c_off:c_off + SSD_STATE].astype(BF16)
            b_rep = jnp.concatenate([b_g] * HEADS_PER_GROUP, axis=0)
            cb = lax.dot_general(c_g, b_rep, (((1,), (1,)), ((), ())), preferred_element_type=F32)
            groups.append((b_f32, c_g, cb))
        shared[c] = (xs, xs * dt, a_cs, a_cs_row, groups)

    def second(c):
        r0 = c * L
        xs, xdt, a_cs, a_cs_row, groups = shared.pop(c)
        decay_in = jnp.exp2(jnp.where(shared["lower"], a_cs - a_cs_row, -jnp.inf))
        a_last = a_cs[L - 1:L, :]
        decay_to_end = jnp.exp2(a_last - a_cs)
        decay_from_start = jnp.exp2(a_cs)
        chunk_decay = jnp.exp2(a_last)
        for g in range(SSD_GROUPS):
            gl = slice(g * GROUP_WIDTH, (g + 1) * GROUP_WIDTH)
            b_f32, c_g, cb = groups[g]
            scores = (cb * decay_in[:, gl]).astype(BF16)
            x_g = xdt[:, gl]
            x_rep = jnp.concatenate([x_g] * HEADS_PER_GROUP, axis=0)
            x_diag = jnp.where(shared["same_head"], x_rep, 0.0).astype(BF16)
            y_diag = jnp.dot(scores, x_diag, preferred_element_type=F32)
            state = state_scr[g]
            y_off = jnp.dot(c_g, state.astype(BF16), preferred_element_type=F32) * decay_from_start[:, gl]
            x_dec = (x_g * decay_to_end[:, gl]).astype(BF16)
            new_state = jnp.dot(b_f32.T.astype(BF16), x_dec, preferred_element_type=F32)
            state_scr[g] = state * chunk_decay[:, gl] + new_state
            y_scr[pl.ds(r0, L), gl] = y_diag + y_off + dskip_ref[:, gl] * xs[:, gl]

    def finish():
        y = y_scr[...] * _silu(z_ref[...])
        for g in range(SSD_GROUPS):
            gl = slice(g * GROUP_WIDTH, (g + 1) * GROUP_WIDTH)
            yg = y[:, gl]
            yg = yg * lax.rsqrt(jnp.mean(yg * yg, axis=-1, keepdims=True) + EPS)
            o_ref[:, gl] = (yg * normw_ref[:, gl]).astype(o_ref.dtype)

    stages = [prepare]
    for c in range(rows // L):
        stages += [functools.partial(first, c), functools.partial(second, c)]
    return stages + [finish]


def _mlp_body(h_ref, ya_ref, ys_ref, wout_ref, postmix_ref, premlp_ref, wup_ref, wdn_ref, postmlp_ref, o_ref):
    mixed = (jnp.dot(ya_ref[...], wout_ref[0:MLA_WIDTH, :], preferred_element_type=F32)
             + jnp.dot(ys_ref[...], wout_ref[MLA_WIDTH:, :], preferred_element_type=F32))
    h1 = h_ref[...] + _rms(mixed, postmix_ref[...])
    m = _rms(h1, premlp_ref[...]).astype(BF16)
    acc = None
    for c in range(D_FF // FF_BLOCK):
        ff = slice(c * FF_BLOCK, (c + 1) * FF_BLOCK)
        up = jnp.dot(m, wup_ref[:, ff], preferred_element_type=F32)
        act = jnp.square(jnp.maximum(up, 0.0)).astype(BF16)
        part = jnp.dot(act, wdn_ref[ff, :], preferred_element_type=F32)
        acc = part if acc is None else acc + part
    o_ref[...] = h1 + _rms(acc, postmlp_ref[...])


def _mlp_call(h, ya, ys, wout, postmix, premlp, wup, wdn, postmlp):
    T = h.shape[0]
    rows = MLP_ROWS
    row_spec = lambda cols: pl.BlockSpec((rows, cols), lambda i: (i, 0))
    return pl.pallas_call(
        _mlp_body,
        grid=(T // rows,),
        in_specs=[row_spec(D_MODEL), row_spec(MLA_WIDTH), row_spec(SSD_INNER),
                  _const_spec((MLA_WIDTH + SSD_INNER, D_MODEL)), _const_spec((1, D_MODEL)),
                  _const_spec((1, D_MODEL)), _const_spec((D_MODEL, D_FF)), _const_spec((D_FF, D_MODEL)),
                  _const_spec((1, D_MODEL))],
        out_specs=row_spec(D_MODEL),
        out_shape=jax.ShapeDtypeStruct((T, D_MODEL), F32),
        compiler_params=pltpu.CompilerParams(dimension_semantics=("parallel",), vmem_limit_bytes=VMEM_LIMIT),
        name="mlp",
    )(h, ya, ys, wout, postmix, premlp, wup, wdn, postmlp)


def _pack_w_in(w_in):
    s1 = Q_LORA_RANK
    s2 = s1 + KV_LORA_RANK
    s3 = s2 + QK_ROPE_DIM
    s4 = s3 + SSD_INNER
    s5 = s4 + CONV_DIM
    zeros = lambda n: jnp.zeros((D_MODEL, n), w_in.dtype)
    kr_dt = jnp.concatenate([zeros(ROPE_LO), w_in[:, s2:s3], w_in[:, s5:],
                             zeros(HEAD_PAD - DT_LANE0 - SSD_HEADS)], axis=1)
    return jnp.concatenate([w_in[:, :s2], kr_dt, w_in[:, s3:s5]], axis=1).astype(BF16)


def _pack_w_uq(w_uq):
    w = w_uq.reshape(Q_LORA_RANK, MLA_HEADS, QK_NOPE_DIM + QK_ROPE_DIM)
    pad = jnp.zeros((Q_LORA_RANK, MLA_HEADS, HEAD_PAD - ROPE_HI), w_uq.dtype)
    return jnp.concatenate([w, pad], axis=-1).reshape(Q_LORA_RANK, MLA_HEADS * HEAD_PAD).astype(BF16)


def _pack_w_ukv(w_ukv):
    w = w_ukv.reshape(KV_LORA_RANK, MLA_HEADS, QK_NOPE_DIM + V_HEAD_DIM)
    pad = jnp.zeros((KV_LORA_RANK, MLA_HEADS, HEAD_PAD - QK_NOPE_DIM), w_ukv.dtype)
    w_uk = jnp.concatenate([w[..., :QK_NOPE_DIM], pad], axis=-1).reshape(KV_LORA_RANK, MLA_HEADS * HEAD_PAD)
    w_uv = w[..., QK_NOPE_DIM:].reshape(KV_LORA_RANK, MLA_WIDTH)
    return w_uk.astype(BF16), w_uv.astype(BF16)


def _per_lane(v):
    return jnp.repeat(v.astype(F32), SSD_HEAD_DIM)[None, :]


def kernel(x, positions, pre_mix_norm, w_in, q_norm, w_uq, kv_norm, w_ukv, conv_w, conv_b, dt_bias, a_log,
           d_skip, ssd_norm, w_out, post_mix_norm, pre_mlp_norm, w_up, w_down, post_mlp_norm):
    batch, seq, _ = x.shape
    depth = w_in.shape[0]
    T = batch * seq
    assert seq % ATTN_ROWS == 0 and seq % PROJ_ROWS == 0 and T % MLP_ROWS == 0

    inv_freq = ROPE_THETA ** (-jnp.arange(0, QK_ROPE_DIM, 2, dtype=F32) / QK_ROPE_DIM)
    invf = jnp.concatenate([jnp.zeros((ROPE_LO,), F32), inv_freq, inv_freq,
                            jnp.zeros((HEAD_PAD - ROPE_HI,), F32)])[None, :]
    pos_b = jnp.broadcast_to(positions.reshape(T, 1), (T, LANES))
    row = lambda v: v.astype(F32)[None, :]

    h = x.reshape(T, D_MODEL)
    for l in range(depth):
        w_uk, w_uv = _pack_w_ukv(w_ukv[l])
        dt_bias_lanes = jnp.zeros((1, LANES), F32).at[0, DT_LANE0:DT_LANE0 + SSD_HEADS].set(
            dt_bias[l].astype(F32))
        q, k, vt, y_ssd = _proj_call(h, pos_b, invf, row(pre_mix_norm[l]), _pack_w_in(w_in[l]),
                                     row(q_norm[l]), _pack_w_uq(w_uq[l]), row(kv_norm[l]), w_uk, w_uv,
                                     conv_w[l].astype(F32), row(conv_b[l]), dt_bias_lanes,
                                     _per_lane(a_log[l]), _per_lane(d_skip[l]), row(ssd_norm[l]), batch, seq)
        y_att = _attn_call(q, k, vt, batch, seq)
        h = _mlp_call(h, y_att, y_ssd, w_out[l].astype(BF16), row(post_mix_norm[l]), row(pre_mlp_norm[l]),
                      w_up[l].astype(BF16), w_down[l].astype(BF16), row(post_mlp_norm[l]))
    return h.reshape(batch, seq, D_MODEL)
```

```python
import functools
import math

import jax
import jax.numpy as jnp
from jax import lax
from jax.experimental import pallas as pl
from jax.experimental.pallas import tpu as pltpu

F32 = jnp.float32
BF16 = jnp.bfloat16

D_MODEL = 1024
CHUNK = 64
EPS = 1e-6
MLA_HEADS = 8
QK_NOPE_DIM = 64
QK_ROPE_DIM = 32
V_HEAD_DIM = 64
Q_LORA_RANK = 768
KV_LORA_RANK = 256
ROPE_THETA = 10000.0
MLA_WIDTH = MLA_HEADS * V_HEAD_DIM
SSD_HEADS = 8
SSD_HEAD_DIM = 64
SSD_INNER = SSD_HEADS * SSD_HEAD_DIM
SSD_GROUPS = 2
SSD_STATE = 128
CONV_WIDTH = 4
CONV_DIM = SSD_INNER + 2 * SSD_GROUPS * SSD_STATE
D_FF = 4 * D_MODEL

LANES = 128
SUBLANES = 8
HEAD_PAD = LANES
HALF_ROPE = QK_ROPE_DIM // 2
ROPE_LO = QK_NOPE_DIM
ROPE_MID = QK_NOPE_DIM + HALF_ROPE
ROPE_HI = QK_NOPE_DIM + QK_ROPE_DIM
HEADS_PER_GROUP = SSD_HEADS // SSD_GROUPS
GROUP_WIDTH = HEADS_PER_GROUP * SSD_HEAD_DIM

C_Q0, C_Q1 = 0, Q_LORA_RANK
C_KV0, C_KV1 = C_Q1, C_Q1 + KV_LORA_RANK
C_KR0, C_KR1 = C_KV1, C_KV1 + HEAD_PAD
C_Z0, C_Z1 = C_KR1, C_KR1 + SSD_INNER
C_X0, C_X1 = C_Z1, C_Z1 + CONV_DIM
PROJ_COLS = C_X1
DT_LANE0 = ROPE_HI

BF16_SUBLANES = 2 * SUBLANES
VT_ROWS = V_HEAD_DIM + BF16_SUBLANES

ATTN_ROWS = 512
ATTN_HEADS = 4
PROJ_ROWS = ATTN_ROWS
MLP_ROWS = PROJ_ROWS
MLP_SLAB = 2 * LANES
FF_BLOCK = 1024
VMEM_LIMIT = 56 * 1024 * 1024


def _rms(x, w):
    return x * lax.rsqrt(jnp.mean(x * x, axis=-1, keepdims=True) + EPS) * w


def _silu(x):
    half = 0.5 * x
    return half + half * jnp.tanh(half)


def _split3(x):
    hi = x.astype(BF16)
    rest = x - hi.astype(F32)
    mid = rest.astype(BF16)
    return hi, mid, (rest - mid.astype(F32)).astype(BF16)


def _const_spec(shape):
    zeros = (0,) * len(shape)
    return pl.BlockSpec(shape, lambda *_: zeros, pipeline_mode=pl.Buffered(1))


def _proj_body(h_ref, pos_ref, invf_ref, prew_ref, w1_ref, qnw_ref, wuq_ref, kvnw_ref, wuk_ref, wuv_ref,
               q_ref, k_ref, vt_ref, z_ref, xbc_ref, dt_ref):
    u = _rms(h_ref[...], prew_ref[...]).astype(BF16)

    def seg(c0, c1):
        return jnp.dot(u, w1_ref[:, c0:c1], preferred_element_type=F32)

    c_q = seg(C_Q0, C_Q1)
    c_kv = seg(C_KV0, C_KV1)
    kr_dt = seg(C_KR0, C_KR1)
    z_ref[...] = seg(C_Z0, C_Z1)
    xbc_ref[...] = seg(C_X0, C_X1)
    dt_ref[...] = kr_dt

    lane = lax.broadcasted_iota(jnp.int32, (1, LANES), 1)
    k_rope = jnp.where(lane < ROPE_HI, kr_dt, 0.0)
    first_half = (lane >= ROPE_LO) & (lane < ROPE_MID)
    second_half = (lane >= ROPE_MID) & (lane < ROPE_HI)
    ang = pos_ref[...].astype(F32) * invf_ref[...]
    cos = jnp.where(first_half | second_half, jnp.cos(ang), 1.0)
    sin = jnp.sin(ang)
    sin_first = jnp.where(first_half, -sin, 0.0)
    sin_second = jnp.where(second_half, sin, 0.0)

    def rope(t):
        return (t * cos + pltpu.roll(t, LANES - HALF_ROPE, 1) * sin_first
                + pltpu.roll(t, HALF_ROPE, 1) * sin_second)

    scale = (QK_NOPE_DIM + QK_ROPE_DIM) ** -0.5 * math.log2(math.e)
    qf = jnp.dot(_rms(c_q, qnw_ref[...]).astype(BF16), wuq_ref[...], preferred_element_type=F32)
    c_kv_n = _rms(c_kv, kvnw_ref[...]).astype(BF16)
    kf = jnp.dot(c_kv_n, wuk_ref[...], preferred_element_type=F32)
    k_rope = rope(k_rope)
    for h in range(MLA_HEADS):
        blk = slice(h * HEAD_PAD, (h + 1) * HEAD_PAD)
        q_ref[h] = (rope(qf[:, blk]) * scale).astype(BF16)
        k_ref[h] = (kf[:, blk] + k_rope).astype(BF16)
    v_t = jnp.dot(c_kv_n, wuv_ref[...], preferred_element_type=F32).T
    ones = jnp.ones((BF16_SUBLANES, v_t.shape[1]), BF16)
    for h in range(MLA_HEADS):
        vt_ref[h, 0, 0:V_HEAD_DIM, :] = v_t[h * V_HEAD_DIM:(h + 1) * V_HEAD_DIM, :].astype(BF16)
        vt_ref[h, 0, V_HEAD_DIM:VT_ROWS, :] = ones


def _proj_call(h, pos_b, invf, prew, w1, qnw, wuq, kvnw, wuk, wuv):
    T = h.shape[0]
    rows = PROJ_ROWS
    row_spec = lambda cols: pl.BlockSpec((rows, cols), lambda i: (i, 0))
    head_spec = pl.BlockSpec((MLA_HEADS, rows, HEAD_PAD), lambda i: (0, i, 0))
    return pl.pallas_call(
        _proj_body,
        grid=(T // rows,),
        in_specs=[row_spec(D_MODEL), row_spec(LANES), _const_spec((1, LANES)), _const_spec((1, D_MODEL)),
                  _const_spec((D_MODEL, PROJ_COLS)), _const_spec((1, Q_LORA_RANK)),
                  _const_spec((Q_LORA_RANK, MLA_HEADS * HEAD_PAD)), _const_spec((1, KV_LORA_RANK)),
                  _const_spec((KV_LORA_RANK, MLA_HEADS * HEAD_PAD)), _const_spec((KV_LORA_RANK, MLA_WIDTH))],
        out_specs=[head_spec, head_spec,
                   pl.BlockSpec((MLA_HEADS, 1, VT_ROWS, rows), lambda i: (0, i, 0, 0)),
                   row_spec(SSD_INNER), row_spec(CONV_DIM), row_spec(LANES)],
        out_shape=[jax.ShapeDtypeStruct((MLA_HEADS, T, HEAD_PAD), BF16),
                   jax.ShapeDtypeStruct((MLA_HEADS, T, HEAD_PAD), BF16),
                   jax.ShapeDtypeStruct((MLA_HEADS, T // rows, VT_ROWS, rows), BF16),
                   jax.ShapeDtypeStruct((T, SSD_INNER), F32),
                   jax.ShapeDtypeStruct((T, CONV_DIM), F32),
                   jax.ShapeDtypeStruct((T, LANES), F32)],
        compiler_params=pltpu.CompilerParams(dimension_semantics=("parallel",), vmem_limit_bytes=VMEM_LIMIT),
        name="proj",
    )(h, pos_b, invf, prew, w1, qnw, wuq, kvnw, wuk, wuv)


def _attn_body(q_ref, k_ref, vt_ref, bias_ref, o_ref, s0_scr, s1_scr, tmax_scr, m_scr, acc_scr, *, rows):
    qi = pl.program_id(2)
    s_scr = (s0_scr, s1_scr)

    def scores(hh, j, slot):
        start = pl.multiple_of(j * rows, rows)
        k = k_ref[hh, pl.ds(start, rows), :]
        s_t = lax.dot_general(k, q_ref[hh], (((1,), (1,)), ((), ())), preferred_element_type=F32)
        s_scr[slot][hh] = s_t
        tmax_scr[slot, hh] = jnp.max(s_t, axis=0, keepdims=True)

    def update(hh, j, slot, diagonal=False):
        s_t = s_scr[slot][hh]
        if diagonal:
            s_t = s_t + bias_ref[...]
            tile_max = jnp.max(s_t, axis=0, keepdims=True)
        else:
            tile_max = tmax_scr[slot, hh]
        m_prev = m_scr[hh]
        m_new = jnp.maximum(m_prev, tile_max)
        p_t = jnp.exp2(s_t - m_new).astype(BF16)
        contrib = jnp.dot(vt_ref[hh, j], p_t, preferred_element_type=F32)
        acc_scr[hh] = acc_scr[hh] * jnp.exp2(m_prev - m_new) + contrib
        m_scr[hh] = m_new

    m_scr[...] = jnp.full_like(m_scr, -jnp.inf)
    acc_scr[...] = jnp.zeros_like(acc_scr)
    for hh in range(ATTN_HEADS):
        scores(hh, 0, 0)

    def step(j, slot, diagonal=False):
        for hh in range(ATTN_HEADS):
            scores(hh, j + 1, 1 - slot)
            update(hh, j, slot, diagonal)

    def tile_pair(i, carry):
        step(2 * i, 0)
        step(2 * i + 1, 1)
        return carry

    lax.fori_loop(0, qi // 2, tile_pair, 0)

    @pl.when(qi % 2 == 0)
    def _():
        for hh in range(ATTN_HEADS):
            update(hh, qi, 0, diagonal=True)

    @pl.when(qi % 2 == 1)
    def _():
        step(qi - 1, 0)
        for hh in range(ATTN_HEADS):
            update(hh, qi, 1, diagonal=True)

    out_t = [acc_scr[hh, 0:V_HEAD_DIM, :] / acc_scr[hh, V_HEAD_DIM:V_HEAD_DIM + 1, :]
             for hh in range(ATTN_HEADS)]
    o_ref[...] = jnp.concatenate(out_t, axis=0).T.astype(o_ref.dtype)


def _attn_call(q, k, vt, batch, seq):
    rows = ATTN_ROWS
    n_q = seq // rows
    groups = MLA_HEADS // ATTN_HEADS
    chunk_of = jnp.arange(rows, dtype=jnp.int32) // CHUNK
    bias = jnp.where(chunk_of[:, None] <= chunk_of[None, :], 0.0, -jnp.inf).astype(F32)
    return pl.pallas_call(
        functools.partial(_attn_body, rows=rows),
        grid=(batch, groups, n_q),
        in_specs=[pl.BlockSpec((ATTN_HEADS, rows, HEAD_PAD), lambda b, g, i: (g, b * n_q + i, 0)),
                  pl.BlockSpec((ATTN_HEADS, seq, HEAD_PAD), lambda b, g, i: (g, b, 0)),
                  pl.BlockSpec((ATTN_HEADS, n_q, VT_ROWS, rows), lambda b, g, i: (g, b, 0, 0)),
                  _const_spec((rows, rows))],
        out_specs=pl.BlockSpec((rows, ATTN_HEADS * V_HEAD_DIM), lambda b, g, i: (b * n_q + i, g)),
        out_shape=jax.ShapeDtypeStruct((batch * seq, MLA_WIDTH), BF16),
        scratch_shapes=[pltpu.VMEM((ATTN_HEADS, rows, rows), F32), pltpu.VMEM((ATTN_HEADS, rows, rows), F32),
                        pltpu.VMEM((2, ATTN_HEADS, 1, rows), F32), pltpu.VMEM((ATTN_HEADS, 1, rows), F32),
                        pltpu.VMEM((ATTN_HEADS, VT_ROWS, rows), F32)],
        compiler_params=pltpu.CompilerParams(dimension_semantics=("parallel", "parallel", "arbitrary"),
                                             vmem_limit_bytes=VMEM_LIMIT),
        name="attn",
    )(q, k, vt, bias)


def _ssd_conv_chunk(c, convw_ref, convb_ref, ext_scr, act_scr):
    tail = SUBLANES
    r0 = c * CHUNK
    window = ext_scr[r0:r0 + tail + CHUNK, :]
    conv = window[tail:, :] * convw_ref[CONV_WIDTH - 1:CONV_WIDTH, :]
    for back in range(1, CONV_WIDTH):
        w_row = convw_ref[CONV_WIDTH - 1 - back:CONV_WIDTH - back, :]
        conv = conv + pltpu.roll(window, back, 0)[tail:, :] * w_row
    act_scr[r0:r0 + CHUNK, :] = _silu(conv + convb_ref[...])


def _ssd_scan_stages(z_ref, dtr_ref, dtb_ref, expand_ref, alog_ref, dskip_ref, normw_ref, o_ref,
                     act_scr, dts_scr, y_scr, state_scr, rows):
    L = CHUNK
    shared = {}

    def prepare():
        dt_heads = jax.nn.softplus(dtr_ref[...] + dtb_ref[...])
        dts_scr[...] = jnp.dot(jnp.concatenate(_split3(dt_heads), axis=1), expand_ref[...],
                               preferred_element_type=F32)
        shared["a_coef"] = -jnp.exp(alog_ref[...]) * math.log2(math.e)
        row_i = lax.broadcasted_iota(jnp.int32, (L, SSD_INNER), 0)
        col_j = lax.broadcasted_iota(jnp.int32, (L, SSD_INNER), 1) % SSD_HEAD_DIM
        shared["upper"] = (row_i <= col_j).astype(F32)
        shared["lower"] = col_j <= row_i
        tril = (lax.broadcasted_iota(jnp.int32, (L, L), 1)
                <= lax.broadcasted_iota(jnp.int32, (L, L), 0)).astype(BF16)
        shared["tril3"] = jnp.concatenate([tril] * 3, axis=1)
        blk_r = lax.broadcasted_iota(jnp.int32, (GROUP_WIDTH, GROUP_WIDTH), 0) // SSD_HEAD_DIM
        blk_c = lax.broadcasted_iota(jnp.int32, (GROUP_WIDTH, GROUP_WIDTH), 1) // SSD_HEAD_DIM
        shared["same_head"] = blk_r == blk_c

    def first(c):
        r0 = c * L
        xs = act_scr[pl.ds(r0, L), 0:SSD_INNER]
        dt = dts_scr[pl.ds(r0, L), :]
        a = dt * shared["a_coef"]
        a_cs = jnp.dot(shared["tril3"], jnp.concatenate(_split3(a), axis=0), preferred_element_type=F32)
        a_cs_row = jnp.sum(a * shared["upper"], axis=0, keepdims=True)
        groups = []
        for g in range(SSD_GROUPS):
            b_off = SSD_INNER + g * SSD_STATE
            c_off = SSD_INNER + SSD_GROUPS * SSD_STATE + g * SSD_STATE
            b_f32 = act_scr[pl.ds(r0, L), b_off:b_off + SSD_STATE]
            b_g = b_f32.astype(BF16)
            c_g = act_scr[pl.ds(r0, L), c_off:c_off + SSD_STATE].astype(BF16)
            b_rep = jnp.concatenate([b_g] * HEADS_PER_GROUP, axis=0)
            cb = lax.dot_general(c_g, b_rep, (((1,), (1,)), ((), ())), preferred_element_type=F32)
            groups.append((b_f32, c_g, cb))
        shared[c] = (xs, xs * dt, a_cs, a_cs_row, groups)

    def second(c):
        r0 = c * L
        xs, xdt, a_cs, a_cs_row, groups = shared.pop(c)
        decay_in = jnp.exp2(jnp.where(shared["lower"], a_cs - a_cs_row, -jnp.inf))
        a_last = a_cs[L - 1:L, :]
        decay_to_end = jnp.exp2(a_last - a_cs)
        decay_from_start = jnp.exp2(a_cs)
        chunk_decay = jnp.exp2(a_last)
        for g in range(SSD_GROUPS):
            gl = slice(g * GROUP_WIDTH, (g + 1) * GROUP_WIDTH)
            b_f32, c_g, cb = groups[g]
            scores = (cb * decay_in[:, gl]).astype(BF16)
            x_g = xdt[:, gl]
            x_rep = jnp.concatenate([x_g] * HEADS_PER_GROUP, axis=0)
            x_diag = jnp.where(shared["same_head"], x_rep, 0.0).astype(BF16)
            y_diag = jnp.dot(scores, x_diag, preferred_element_type=F32)
            state = state_scr[g]
            y_off = jnp.dot(c_g, state.astype(BF16), preferred_element_type=F32) * decay_from_start[:, gl]
            x_dec = (x_g * decay_to_end[:, gl]).astype(BF16)
            new_state = jnp.dot(b_f32.T.astype(BF16), x_dec, preferred_element_type=F32)
            state_scr[g] = state * chunk_decay[:, gl] + new_state
            y_scr[pl.ds(r0, L), gl] = y_diag + y_off + dskip_ref[:, gl] * xs[:, gl]

    def finish():
        y = y_scr[...] * _silu(z_ref[...])
        for g in range(SSD_GROUPS):
            gl = slice(g * GROUP_WIDTH, (g + 1) * GROUP_WIDTH)
            yg = y[:, gl]
            yg = yg * lax.rsqrt(jnp.mean(yg * yg, axis=-1, keepdims=True) + EPS)
            o_ref[:, gl] = (yg * normw_ref[:, gl]).astype(o_ref.dtype)

    stages = [prepare]
    for c in range(rows // L):
        stages += [functools.partial(first, c), functools.partial(second, c)]
    return stages + [finish]


def _mlp_body(h_ref, ya_ref, z_ref, xbc_ref, dt_ref, wout_ref, postmix_ref, premlp_ref, wup_ref, wdn_ref,
              postmlp_ref, convw_ref, convb_ref, dtb_ref, expand_ref, alog_ref, dskip_ref, normw_ref,
              o_ref, ext_scr, act_scr, dts_scr, y_scr, ys_scr, state_scr, *, rows, tiles_per_seq):
    j = pl.program_id(0)

    @pl.when(j == 0)
    def _():
        ys_scr[...] = jnp.zeros_like(ys_scr)

    @pl.when(j % tiles_per_seq == 0)
    def _():
        ext_scr[0:SUBLANES, :] = jnp.zeros((SUBLANES, CONV_DIM), F32)
        state_scr[...] = jnp.zeros_like(state_scr)

    ys_prev = ys_scr[...]
    ext_scr[SUBLANES:SUBLANES + rows, :] = xbc_ref[...]
    scan = _ssd_scan_stages(z_ref, dt_ref, dtb_ref, expand_ref, alog_ref, dskip_ref, normw_ref, ys_scr,
                            act_scr, dts_scr, y_scr, state_scr, rows)
    scan[0]()
    pending = []
    for c in range(rows // CHUNK):
        pending += [functools.partial(_ssd_conv_chunk, c, convw_ref, convb_ref, ext_scr, act_scr),
                    scan[1 + 2 * c], scan[2 + 2 * c]]
    pending.append(scan[-1])
    n_stage = len(pending)
    paced_slabs = (D_FF // FF_BLOCK) * (FF_BLOCK // MLP_SLAB + D_MODEL // MLP_SLAB) - 2
    slabs_done = [0]

    def mixer_stage():
        slabs_done[0] += 1
        target = -(-n_stage * slabs_done[0] // paced_slabs)
        while pending and n_stage - len(pending) < target:
            pending.pop(0)()

    mixed = (jnp.dot(ya_ref[...], wout_ref[0:MLA_WIDTH, :], preferred_element_type=F32)
             + jnp.dot(ys_prev, wout_ref[MLA_WIDTH:, :], preferred_element_type=F32))
    h1 = h_ref[...] + _rms(mixed, postmix_ref[...])
    m = _rms(h1, premlp_ref[...]).astype(BF16)
    n_slab = FF_BLOCK // MLP_SLAB
    acc = [None] * (D_MODEL // MLP_SLAB)
    for c in range(D_FF // FF_BLOCK):
        ups = []
        for s in range(n_slab):
            c0 = c * FF_BLOCK + s * MLP_SLAB
            ups.append(jnp.dot(m, wup_ref[:, c0:c0 + MLP_SLAB], preferred_element_type=F32))
            mixer_stage()
        act = jnp.square(jnp.maximum(jnp.concatenate(ups, axis=1), 0.0)).astype(BF16)
        for s in range(len(acc)):
            part = jnp.dot(act, wdn_ref[c * FF_BLOCK:(c + 1) * FF_BLOCK, s * MLP_SLAB:(s + 1) * MLP_SLAB],
                           preferred_element_type=F32)
            acc[s] = part if acc[s] is None else acc[s] + part
            mixer_stage()
    assert not pending
    ext_scr[0:SUBLANES, :] = ext_scr[rows:rows + SUBLANES, :]
    o_ref[...] = h1 + _rms(jnp.concatenate(acc, axis=1), postmlp_ref[...])


def _mlp_call(h, ya, z, xbc, dt, wout, postmix, premlp, wup, wdn, postmlp, convw, convb, dtb, alog, dskip,
              normw, seq):
    T = h.shape[0]
    rows = MLP_ROWS
    n_tiles = T // rows
    mlp_spec = lambda cols: pl.BlockSpec((rows, cols), lambda j: (jnp.maximum(j - 1, 0), 0))
    ssd_spec = lambda cols: pl.BlockSpec((rows, cols), lambda j: (jnp.minimum(j, n_tiles - 1), 0))
    src_head = jnp.arange(LANES, dtype=jnp.int32)[:, None] - DT_LANE0
    dst_head = jnp.arange(SSD_INNER, dtype=jnp.int32)[None, :] // SSD_HEAD_DIM
    expand = jnp.tile((src_head == dst_head).astype(BF16), (3, 1))
    return pl.pallas_call(
        functools.partial(_mlp_body, rows=rows, tiles_per_seq=seq // rows),
        grid=(n_tiles + 1,),
        in_specs=[mlp_spec(D_MODEL), mlp_spec(MLA_WIDTH), ssd_spec(SSD_INNER), ssd_spec(CONV_DIM),
                  ssd_spec(LANES),
                  _const_spec((MLA_WIDTH + SSD_INNER, D_MODEL)), _const_spec((1, D_MODEL)),
                  _const_spec((1, D_MODEL)), _const_spec((D_MODEL, D_FF)), _const_spec((D_FF, D_MODEL)),
                  _const_spec((1, D_MODEL)),
                  _const_spec((CONV_WIDTH, CONV_DIM)), _const_spec((1, CONV_DIM)), _const_spec((1, LANES)),
                  _const_spec((3 * LANES, SSD_INNER)),
                  _const_spec((1, SSD_INNER)), _const_spec((1, SSD_INNER)), _const_spec((1, SSD_INNER))],
        out_specs=mlp_spec(D_MODEL),
        out_shape=jax.ShapeDtypeStruct((T, D_MODEL), F32),
        scratch_shapes=[pltpu.VMEM((rows + SUBLANES, CONV_DIM), F32), pltpu.VMEM((rows, CONV_DIM), F32),
                        pltpu.VMEM((rows, SSD_INNER), F32), pltpu.VMEM((rows, SSD_INNER), F32),
                        pltpu.VMEM((rows, SSD_INNER), BF16),
                        pltpu.VMEM((SSD_GROUPS, SSD_STATE, GROUP_WIDTH), F32)],
        compiler_params=pltpu.CompilerParams(dimension_semantics=("arbitrary",), vmem_limit_bytes=VMEM_LIMIT),
        name="mix_mlp",
    )(h, ya, z, xbc, dt, wout, postmix, premlp, wup, wdn, postmlp, convw, convb, dtb, expand, alog, dskip,
      normw)


def _pack_w_in(w_in):
    s1 = Q_LORA_RANK
    s2 = s1 + KV_LORA_RANK
    s3 = s2 + QK_ROPE_DIM
    s4 = s3 + SSD_INNER
    s5 = s4 + CONV_DIM
    zeros = lambda n: jnp.zeros((D_MODEL, n), w_in.dtype)
    kr_dt = jnp.concatenate([zeros(ROPE_LO), w_in[:, s2:s3], w_in[:, s5:],
                             zeros(HEAD_PAD - DT_LANE0 - SSD_HEADS)], axis=1)
    return jnp.concatenate([w_in[:, :s2], kr_dt, w_in[:, s3:s5]], axis=1).astype(BF16)


def _pack_w_uq(w_uq):
    w = w_uq.reshape(Q_LORA_RANK, MLA_HEADS, QK_NOPE_DIM + QK_ROPE_DIM)
    pad = jnp.zeros((Q_LORA_RANK, MLA_HEADS, HEAD_PAD - ROPE_HI), w_uq.dtype)
    return jnp.concatenate([w, pad], axis=-1).reshape(Q_LORA_RANK, MLA_HEADS * HEAD_PAD).astype(BF16)


def _pack_w_ukv(w_ukv):
    w = w_ukv.reshape(KV_LORA_RANK, MLA_HEADS, QK_NOPE_DIM + V_HEAD_DIM)
    pad = jnp.zeros((KV_LORA_RANK, MLA_HEADS, HEAD_PAD - QK_NOPE_DIM), w_ukv.dtype)
    w_uk = jnp.concatenate([w[..., :QK_NOPE_DIM], pad], axis=-1).reshape(KV_LORA_RANK, MLA_HEADS * HEAD_PAD)
    w_uv = w[..., QK_NOPE_DIM:].reshape(KV_LORA_RANK, MLA_WIDTH)
    return w_uk.astype(BF16), w_uv.astype(BF16)


def _per_lane(v):
    return jnp.repeat(v.astype(F32), SSD_HEAD_DIM)[None, :]


def kernel(x, positions, pre_mix_norm, w_in, q_norm, w_uq, kv_norm, w_ukv, conv_w, conv_b, dt_bias, a_log,
           d_skip, ssd_norm, w_out, post_mix_norm, pre_mlp_norm, w_up, w_down, post_mlp_norm):
    batch, seq, _ = x.shape
    depth = w_in.shape[0]
    T = batch * seq
    assert seq % ATTN_ROWS == 0 and seq % PROJ_ROWS == 0 and T % MLP_ROWS == 0

    inv_freq = ROPE_THETA ** (-jnp.arange(0, QK_ROPE_DIM, 2, dtype=F32) / QK_ROPE_DIM)
    invf = jnp.concatenate([jnp.zeros((ROPE_LO,), F32), inv_freq, inv_freq,
                            jnp.zeros((HEAD_PAD - ROPE_HI,), F32)])[None, :]
    pos_b = jnp.broadcast_to(positions.reshape(T, 1), (T, LANES))
    row = lambda v: v.astype(F32)[None, :]

    h = x.reshape(T, D_MODEL)
    for l in range(depth):
        w_uk, w_uv = _pack_w_ukv(w_ukv[l])
        dt_bias_lanes = jnp.zeros((1, LANES), F32).at[0, DT_LANE0:DT_LANE0 + SSD_HEADS].set(
            dt_bias[l].astype(F32))
        q, k, vt, z, xbc, dt = _proj_call(h, pos_b, invf, row(pre_mix_norm[l]), _pack_w_in(w_in[l]),
                                         row(q_norm[l]), _pack_w_uq(w_uq[l]), row(kv_norm[l]), w_uk, w_uv)
        y_att = _attn_call(q, k, vt, batch, seq)
        h = _mlp_call(h, y_att, z, xbc, dt, w_out[l].astype(BF16), row(post_mix_norm[l]),
                      row(pre_mlp_norm[l]), w_up[l].astype(BF16), w_down[l].astype(BF16),
                      row(post_mlp_norm[l]), conv_w[l].astype(F32), row(conv_b[l]), dt_bias_lanes,
                      _per_lane(a_log[l]), _per_lane(d_skip[l]), row(ssd_norm[l]), seq)
    return h.reshape(batch, seq, D_MODEL)
```

```python
import functools
import math

import jax
import jax.numpy as jnp
from jax import lax
from jax.experimental import pallas as pl
from jax.experimental.pallas import tpu as pltpu

F32 = jnp.float32
BF16 = jnp.bfloat16

D_MODEL = 1024
CHUNK = 64
EPS = 1e-6
MLA_HEADS = 8
QK_NOPE_DIM = 64
QK_ROPE_DIM = 32
V_HEAD_DIM = 64
Q_LORA_RANK = 768
KV_LORA_RANK = 256
ROPE_THETA = 10000.0
MLA_WIDTH = MLA_HEADS * V_HEAD_DIM
SSD_HEADS = 8
SSD_HEAD_DIM = 64
SSD_INNER = SSD_HEADS * SSD_HEAD_DIM
SSD_GROUPS = 2
SSD_STATE = 128
CONV_WIDTH = 4
CONV_DIM = SSD_INNER + 2 * SSD_GROUPS * SSD_STATE
D_FF = 4 * D_MODEL

LANES = 128
SUBLANES = 8
HEAD_PAD = LANES
HALF_ROPE = QK_ROPE_DIM // 2
ROPE_LO = QK_NOPE_DIM
ROPE_MID = QK_NOPE_DIM + HALF_ROPE
ROPE_HI = QK_NOPE_DIM + QK_ROPE_DIM
HEADS_PER_GROUP = SSD_HEADS // SSD_GROUPS
GROUP_WIDTH = HEADS_PER_GROUP * SSD_HEAD_DIM

C_Q0, C_Q1 = 0, Q_LORA_RANK
C_KV0, C_KV1 = C_Q1, C_Q1 + KV_LORA_RANK
C_KR0, C_KR1 = C_KV1, C_KV1 + HEAD_PAD
C_Z0, C_Z1 = C_KR1, C_KR1 + SSD_INNER
C_X0, C_X1 = C_Z1, C_Z1 + CONV_DIM
PROJ_COLS = C_X1
DT_LANE0 = ROPE_HI

BF16_SUBLANES = 2 * SUBLANES
VT_ROWS = V_HEAD_DIM + BF16_SUBLANES

ATTN_ROWS = 512
ATTN_HEADS = 8
PROJ_ROWS = ATTN_ROWS
MLP_ROWS = PROJ_ROWS
MLP_SLAB = 2 * LANES
FF_BLOCK = 1024
VMEM_LIMIT = 56 * 1024 * 1024


def _rms(x, w):
    return x * lax.rsqrt(jnp.mean(x * x, axis=-1, keepdims=True) + EPS) * w


def _silu(x):
    half = 0.5 * x
    return half + half * jnp.tanh(half)


def _split3(x):
    hi = x.astype(BF16)
    rest = x - hi.astype(F32)
    mid = rest.astype(BF16)
    return hi, mid, (rest - mid.astype(F32)).astype(BF16)


def _const_spec(shape):
    zeros = (0,) * len(shape)
    return pl.BlockSpec(shape, lambda *_: zeros, pipeline_mode=pl.Buffered(1))


def _proj_body(h_ref, pos_ref, invf_ref, prew_ref, w1_ref, qnw_ref, wuq_ref, kvnw_ref, wuk_ref, wuv_ref,
               q_ref, k_ref, vt_ref, z_ref, xbc_ref, dt_ref):
    u = _rms(h_ref[...], prew_ref[...]).astype(BF16)

    def seg(c0, c1):
        return jnp.dot(u, w1_ref[:, c0:c1], preferred_element_type=F32)

    c_q = seg(C_Q0, C_Q1)
    c_kv = seg(C_KV0, C_KV1)
    kr_dt = seg(C_KR0, C_KR1)
    z_ref[...] = seg(C_Z0, C_Z1)
    xbc_ref[...] = seg(C_X0, C_X1)
    dt_ref[...] = kr_dt

    lane = lax.broadcasted_iota(jnp.int32, (1, LANES), 1)
    k_rope = jnp.where(lane < ROPE_HI, kr_dt, 0.0)
    first_half = (lane >= ROPE_LO) & (lane < ROPE_MID)
    second_half = (lane >= ROPE_MID) & (lane < ROPE_HI)
    ang = pos_ref[...].astype(F32) * invf_ref[...]
    cos = jnp.where(first_half | second_half, jnp.cos(ang), 1.0)
    sin = jnp.sin(ang)
    sin_first = jnp.where(first_half, -sin, 0.0)
    sin_second = jnp.where(second_half, sin, 0.0)

    def rope(t):
        return (t * cos + pltpu.roll(t, LANES - HALF_ROPE, 1) * sin_first
                + pltpu.roll(t, HALF_ROPE, 1) * sin_second)

    scale = (QK_NOPE_DIM + QK_ROPE_DIM) ** -0.5 * math.log2(math.e)
    qf = jnp.dot(_rms(c_q, qnw_ref[...]).astype(BF16), wuq_ref[...], preferred_element_type=F32)
    c_kv_n = _rms(c_kv, kvnw_ref[...]).astype(BF16)
    kf = jnp.dot(c_kv_n, wuk_ref[...], preferred_element_type=F32)
    k_rope = rope(k_rope)
    for h in range(MLA_HEADS):
        blk = slice(h * HEAD_PAD, (h + 1) * HEAD_PAD)
        q_ref[h] = (rope(qf[:, blk]) * scale).astype(BF16)
        k_ref[h] = (kf[:, blk] + k_rope).astype(BF16)
    v_t = jnp.dot(c_kv_n, wuv_ref[...], preferred_element_type=F32).T
    ones = jnp.ones((BF16_SUBLANES, v_t.shape[1]), BF16)
    for h in range(MLA_HEADS):
        vt_ref[h, 0, 0:V_HEAD_DIM, :] = v_t[h * V_HEAD_DIM:(h + 1) * V_HEAD_DIM, :].astype(BF16)
        vt_ref[h, 0, V_HEAD_DIM:VT_ROWS, :] = ones


def _proj_call(h, pos_b, invf, prew, w1, qnw, wuq, kvnw, wuk, wuv):
    T = h.shape[0]
    rows = PROJ_ROWS
    row_spec = lambda cols: pl.BlockSpec((rows, cols), lambda i: (i, 0))
    head_spec = pl.BlockSpec((MLA_HEADS, rows, HEAD_PAD), lambda i: (0, i, 0))
    return pl.pallas_call(
        _proj_body,
        grid=(T // rows,),
        in_specs=[row_spec(D_MODEL), row_spec(LANES), _const_spec((1, LANES)), _const_spec((1, D_MODEL)),
                  _const_spec((D_MODEL, PROJ_COLS)), _const_spec((1, Q_LORA_RANK)),
                  _const_spec((Q_LORA_RANK, MLA_HEADS * HEAD_PAD)), _const_spec((1, KV_LORA_RANK)),
                  _const_spec((KV_LORA_RANK, MLA_HEADS * HEAD_PAD)), _const_spec((KV_LORA_RANK, MLA_WIDTH))],
        out_specs=[head_spec, head_spec,
                   pl.BlockSpec((MLA_HEADS, 1, VT_ROWS, rows), lambda i: (0, i, 0, 0)),
                   row_spec(SSD_INNER), row_spec(CONV_DIM), row_spec(LANES)],
        out_shape=[jax.ShapeDtypeStruct((MLA_HEADS, T, HEAD_PAD), BF16),
                   jax.ShapeDtypeStruct((MLA_HEADS, T, HEAD_PAD), BF16),
                   jax.ShapeDtypeStruct((MLA_HEADS, T // rows, VT_ROWS, rows), BF16),
                   jax.ShapeDtypeStruct((T, SSD_INNER), F32),
                   jax.ShapeDtypeStruct((T, CONV_DIM), F32),
                   jax.ShapeDtypeStruct((T, LANES), F32)],
        compiler_params=pltpu.CompilerParams(dimension_semantics=("parallel",), vmem_limit_bytes=VMEM_LIMIT),
        name="proj",
    )(h, pos_b, invf, prew, w1, qnw, wuq, kvnw, wuk, wuv)


def _attn_body(q_ref, k_ref, vt_ref, bias_ref, o_ref, s0_scr, s1_scr, tmax_scr, m_scr, acc_scr, *, rows):
    qi = pl.program_id(2)
    s_scr = (s0_scr, s1_scr)

    def scores(hh, j, slot):
        start = pl.multiple_of(j * rows, rows)
        k = k_ref[hh, pl.ds(start, rows), :]
        s_t = lax.dot_general(k, q_ref[hh], (((1,), (1,)), ((), ())), preferred_element_type=F32)
        s_scr[slot][hh] = s_t
        tmax_scr[slot, hh] = jnp.max(s_t, axis=0, keepdims=True)

    def update(hh, j, slot, diagonal=False):
        s_t = s_scr[slot][hh]
        if diagonal:
            s_t = s_t + bias_ref[...]
            tile_max = jnp.max(s_t, axis=0, keepdims=True)
        else:
            tile_max = tmax_scr[slot, hh]
        m_prev = m_scr[hh]
        m_new = jnp.maximum(m_prev, tile_max)
        p_t = jnp.exp2(s_t - m_new).astype(BF16)
        contrib = jnp.dot(vt_ref[hh, j], p_t, preferred_element_type=F32)
        acc_scr[hh] = acc_scr[hh] * jnp.exp2(m_prev - m_new) + contrib
        m_scr[hh] = m_new

    m_scr[...] = jnp.full_like(m_scr, -jnp.inf)
    acc_scr[...] = jnp.zeros_like(acc_scr)
    for hh in range(ATTN_HEADS):
        scores(hh, 0, 0)

    def step(j, slot, diagonal=False):
        for hh in range(ATTN_HEADS):
            scores(hh, j + 1, 1 - slot)
            update(hh, j, slot, diagonal)

    def tile_pair(i, carry):
        step(2 * i, 0)
        step(2 * i + 1, 1)
        return carry

    lax.fori_loop(0, qi // 2, tile_pair, 0)

    @pl.when(qi % 2 == 0)
    def _():
        for hh in range(ATTN_HEADS):
            update(hh, qi, 0, diagonal=True)

    @pl.when(qi % 2 == 1)
    def _():
        step(qi - 1, 0)
        for hh in range(ATTN_HEADS):
            update(hh, qi, 1, diagonal=True)

    out_t = [acc_scr[hh, 0:V_HEAD_DIM, :] / acc_scr[hh, V_HEAD_DIM:V_HEAD_DIM + 1, :]
             for hh in range(ATTN_HEADS)]
    o_ref[...] = jnp.concatenate(out_t, axis=0).T.astype(o_ref.dtype)


def _attn_call(q, k, vt, batch, seq):
    rows = ATTN_ROWS
    n_q = seq // rows
    groups = MLA_HEADS // ATTN_HEADS
    chunk_of = jnp.arange(rows, dtype=jnp.int32) // CHUNK
    bias = jnp.where(chunk_of[:, None] <= chunk_of[None, :], 0.0, -jnp.inf).astype(F32)
    return pl.pallas_call(
        functools.partial(_attn_body, rows=rows),
        grid=(batch, groups, n_q),
        in_specs=[pl.BlockSpec((ATTN_HEADS, rows, HEAD_PAD), lambda b, g, i: (g, b * n_q + i, 0)),
                  pl.BlockSpec((ATTN_HEADS, seq, HEAD_PAD), lambda b, g, i: (g, b, 0),
                               pipeline_mode=pl.Buffered(1)),
                  pl.BlockSpec((ATTN_HEADS, n_q, VT_ROWS, rows), lambda b, g, i: (g, b, 0, 0),
                               pipeline_mode=pl.Buffered(1)),
                  _const_spec((rows, rows))],
        out_specs=pl.BlockSpec((rows, ATTN_HEADS * V_HEAD_DIM), lambda b, g, i: (b * n_q + i, g)),
        out_shape=jax.ShapeDtypeStruct((batch * seq, MLA_WIDTH), BF16),
        scratch_shapes=[pltpu.VMEM((ATTN_HEADS, rows, rows), F32), pltpu.VMEM((ATTN_HEADS, rows, rows), F32),
                        pltpu.VMEM((2, ATTN_HEADS, 1, rows), F32), pltpu.VMEM((ATTN_HEADS, 1, rows), F32),
                        pltpu.VMEM((ATTN_HEADS, VT_ROWS, rows), F32)],
        compiler_params=pltpu.CompilerParams(dimension_semantics=("parallel", "parallel", "arbitrary"),
                                             vmem_limit_bytes=VMEM_LIMIT),
        name="attn",
    )(q, k, vt, bias)


def _ssd_conv_chunk(c, convw_ref, convb_ref, ext_scr, act_scr):
    tail = SUBLANES
    r0 = c * CHUNK
    window = ext_scr[r0:r0 + tail + CHUNK, :]
    conv = window[tail:, :] * convw_ref[CONV_WIDTH - 1:CONV_WIDTH, :]
    for back in range(1, CONV_WIDTH):
        w_row = convw_ref[CONV_WIDTH - 1 - back:CONV_WIDTH - back, :]
        conv = conv + pltpu.roll(window, back, 0)[tail:, :] * w_row
    act_scr[r0:r0 + CHUNK, :] = _silu(conv + convb_ref[...])


def _ssd_scan_stages(z_ref, dtr_ref, dtb_ref, expand_ref, alog_ref, dskip_ref, normw_ref, o_ref,
                     act_scr, dts_scr, y_scr, state_scr, rows):
    L = CHUNK
    shared = {}

    def prepare():
        dt_heads = jax.nn.softplus(dtr_ref[...] + dtb_ref[...])
        dts_scr[...] = jnp.dot(jnp.concatenate(_split3(dt_heads), axis=1), expand_ref[...],
                               preferred_element_type=F32)
        shared["a_coef"] = -jnp.exp(alog_ref[...]) * math.log2(math.e)
        row_i = lax.broadcasted_iota(jnp.int32, (L, SSD_INNER), 0)
        col_j = lax.broadcasted_iota(jnp.int32, (L, SSD_INNER), 1) % SSD_HEAD_DIM
        shared["upper"] = (row_i <= col_j).astype(F32)
        shared["lower"] = col_j <= row_i
        tril = (lax.broadcasted_iota(jnp.int32, (L, L), 1)
                <= lax.broadcasted_iota(jnp.int32, (L, L), 0)).astype(BF16)
        shared["tril3"] = jnp.concatenate([tril] * 3, axis=1)
        blk_r = lax.broadcasted_iota(jnp.int32, (GROUP_WIDTH, GROUP_WIDTH), 0) // SSD_HEAD_DIM
        blk_c = lax.broadcasted_iota(jnp.int32, (GROUP_WIDTH, GROUP_WIDTH), 1) // SSD_HEAD_DIM
        shared["same_head"] = blk_r == blk_c

    def first(c):
        r0 = c * L
        xs = act_scr[pl.ds(r0, L), 0:SSD_INNER]
        dt = dts_scr[pl.ds(r0, L), :]
        a = dt * shared["a_coef"]
        a_cs = jnp.dot(shared["tril3"], jnp.concatenate(_split3(a), axis=0), preferred_element_type=F32)
        a_cs_row = jnp.sum(a * shared["upper"], axis=0, keepdims=True)
        groups = []
        for g in range(SSD_GROUPS):
            b_off = SSD_INNER + g * SSD_STATE
            c_off = SSD_INNER + SSD_GROUPS * SSD_STATE + g * SSD_STATE
            b_f32 = act_scr[pl.ds(r0, L), b_off:b_off + SSD_STATE]
            b_g = b_f32.astype(BF16)
            c_g = act_scr[pl.ds(r0, L), c_off:c_off + SSD_STATE].astype(BF16)
            b_rep = jnp.concatenate([b_g] * HEADS_PER_GROUP, axis=0)
            cb = lax.dot_general(c_g, b_rep, (((1,), (1,)), ((), ())), preferred_element_type=F32)
            groups.append((b_f32, c_g, cb))
        shared[c] = (xs, xs * dt, a_cs, a_cs_row, groups)

    def second(c):
        r0 = c * L
        xs, xdt, a_cs, a_cs_row, groups = shared.pop(c)
        decay_in = jnp.exp2(jnp.where(shared["lower"], a_cs - a_cs_row, -jnp.inf))
        a_last = a_cs[L - 1:L, :]
        decay_to_end = jnp.exp2(a_last - a_cs)
        decay_from_start = jnp.exp2(a_cs)
        chunk_decay = jnp.exp2(a_last)
        for g in range(SSD_GROUPS):
            gl = slice(g * GROUP_WIDTH, (g + 1) * GROUP_WIDTH)
            b_f32, c_g, cb = groups[g]
            scores = (cb * decay_in[:, gl]).astype(BF16)
            x_g = xdt[:, gl]
            x_rep = jnp.concatenate([x_g] * HEADS_PER_GROUP, axis=0)
            x_diag = jnp.where(shared["same_head"], x_rep, 0.0).astype(BF16)
            y_diag = jnp.dot(scores, x_diag, preferred_element_type=F32)
            state = state_scr[g]
            y_off = jnp.dot(c_g, state.astype(BF16), preferred_element_type=F32) * decay_from_start[:, gl]
            x_dec = (x_g * decay_to_end[:, gl]).astype(BF16)
            new_state = jnp.dot(b_f32.T.astype(BF16), x_dec, preferred_element_type=F32)
            state_scr[g] = state * chunk_decay[:, gl] + new_state
            y_scr[pl.ds(r0, L), gl] = y_diag + y_off + dskip_ref[:, gl] * xs[:, gl]

    def finish():
        y = y_scr[...] * _silu(z_ref[...])
        for g in range(SSD_GROUPS):
            gl = slice(g * GROUP_WIDTH, (g + 1) * GROUP_WIDTH)
            yg = y[:, gl]
            yg = yg * lax.rsqrt(jnp.mean(yg * yg, axis=-1, keepdims=True) + EPS)
            o_ref[:, gl] = (yg * normw_ref[:, gl]).astype(o_ref.dtype)

    stages = [prepare]
    for c in range(rows // L):
        stages += [functools.partial(first, c), functools.partial(second, c)]
    return stages + [finish]


def _mlp_body(h_ref, ya_ref, z_ref, xbc_ref, dt_ref, wout_ref, postmix_ref, premlp_ref, wup_ref, wdn_ref,
              postmlp_ref, convw_ref, convb_ref, dtb_ref, expand_ref, alog_ref, dskip_ref, normw_ref,
              o_ref, ext_scr, act_scr, dts_scr, y_scr, ys_scr, state_scr, *, rows, tiles_per_seq):
    j = pl.program_id(0)

    @pl.when(j == 0)
    def _():
        ys_scr[...] = jnp.zeros_like(ys_scr)

    @pl.when(j % tiles_per_seq == 0)
    def _():
        ext_scr[0:SUBLANES, :] = jnp.zeros((SUBLANES, CONV_DIM), F32)
        state_scr[...] = jnp.zeros_like(state_scr)

    ys_prev = ys_scr[...]
    ext_scr[SUBLANES:SUBLANES + rows, :] = xbc_ref[...]
    scan = _ssd_scan_stages(z_ref, dt_ref, dtb_ref, expand_ref, alog_ref, dskip_ref, normw_ref, ys_scr,
                            act_scr, dts_scr, y_scr, state_scr, rows)
    scan[0]()
    pending = []
    for c in range(rows // CHUNK):
        pending += [functools.partial(_ssd_conv_chunk, c, convw_ref, convb_ref, ext_scr, act_scr),
                    scan[1 + 2 * c], scan[2 + 2 * c]]
    pending.append(scan[-1])
    n_stage = len(pending)
    paced_slabs = (D_FF // FF_BLOCK) * (FF_BLOCK // MLP_SLAB + D_MODEL // MLP_SLAB) - 2
    slabs_done = [0]

    def mixer_stage():
        slabs_done[0] += 1
        target = -(-n_stage * slabs_done[0] // paced_slabs)
        while pending and n_stage - len(pending) < target:
            pending.pop(0)()

    mixed = (jnp.dot(ya_ref[...], wout_ref[0:MLA_WIDTH, :], preferred_element_type=F32)
             + jnp.dot(ys_prev, wout_ref[MLA_WIDTH:, :], preferred_element_type=F32))
    h1 = h_ref[...] + _rms(mixed, postmix_ref[...])
    m = _rms(h1, premlp_ref[...]).astype(BF16)
    n_slab = FF_BLOCK // MLP_SLAB
    acc = [None] * (D_MODEL // MLP_SLAB)
    for c in range(D_FF // FF_BLOCK):
        ups = []
        for s in range(n_slab):
            c0 = c * FF_BLOCK + s * MLP_SLAB
            ups.append(jnp.dot(m, wup_ref[:, c0:c0 + MLP_SLAB], preferred_element_type=F32))
            mixer_stage()
        act = jnp.square(jnp.maximum(jnp.concatenate(ups, axis=1), 0.0)).astype(BF16)
        for s in range(len(acc)):
            part = jnp.dot(act, wdn_ref[c * FF_BLOCK:(c + 1) * FF_BLOCK, s * MLP_SLAB:(s + 1) * MLP_SLAB],
                           preferred_element_type=F32)
            acc[s] = part if acc[s] is None else acc[s] + part
            mixer_stage()
    assert not pending
    ext_scr[0:SUBLANES, :] = ext_scr[rows:rows + SUBLANES, :]
    o_ref[...] = h1 + _rms(jnp.concatenate(acc, axis=1), postmlp_ref[...])


def _mlp_call(h, ya, z, xbc, dt, wout, postmix, premlp, wup, wdn, postmlp, convw, convb, dtb, alog, dskip,
              normw, seq):
    T = h.shape[0]
    rows = MLP_ROWS
    n_tiles = T // rows
    mlp_spec = lambda cols: pl.BlockSpec((rows, cols), lambda j: (jnp.maximum(j - 1, 0), 0))
    ssd_spec = lambda cols: pl.BlockSpec((rows, cols), lambda j: (jnp.minimum(j, n_tiles - 1), 0))
    src_head = jnp.arange(LANES, dtype=jnp.int32)[:, None] - DT_LANE0
    dst_head = jnp.arange(SSD_INNER, dtype=jnp.int32)[None, :] // SSD_HEAD_DIM
    expand = jnp.tile((src_head == dst_head).astype(BF16), (3, 1))
    return pl.pallas_call(
        functools.partial(_mlp_body, rows=rows, tiles_per_seq=seq // rows),
        grid=(n_tiles + 1,),
        in_specs=[mlp_spec(D_MODEL), mlp_spec(MLA_WIDTH), ssd_spec(SSD_INNER), ssd_spec(CONV_DIM),
                  ssd_spec(LANES),
                  _const_spec((MLA_WIDTH + SSD_INNER, D_MODEL)), _const_spec((1, D_MODEL)),
                  _const_spec((1, D_MODEL)), _const_spec((D_MODEL, D_FF)), _const_spec((D_FF, D_MODEL)),
                  _const_spec((1, D_MODEL)),
                  _const_spec((CONV_WIDTH, CONV_DIM)), _const_spec((1, CONV_DIM)), _const_spec((1, LANES)),
                  _const_spec((3 * LANES, SSD_INNER)),
                  _const_spec((1, SSD_INNER)), _const_spec((1, SSD_INNER)), _const_spec((1, SSD_INNER))],
        out_specs=mlp_spec(D_MODEL),
        out_shape=jax.ShapeDtypeStruct((T, D_MODEL), F32),
        scratch_shapes=[pltpu.VMEM((rows + SUBLANES, CONV_DIM), F32), pltpu.VMEM((rows, CONV_DIM), F32),
                        pltpu.VMEM((rows, SSD_INNER), F32), pltpu.VMEM((rows, SSD_INNER), F32),
                        pltpu.VMEM((rows, SSD_INNER), BF16),
                        pltpu.VMEM((SSD_GROUPS, SSD_STATE, GROUP_WIDTH), F32)],
        compiler_params=pltpu.CompilerParams(dimension_semantics=("arbitrary",), vmem_limit_bytes=VMEM_LIMIT),
        name="mix_mlp",
    )(h, ya, z, xbc, dt, wout, postmix, premlp, wup, wdn, postmlp, convw, convb, dtb, expand, alog, dskip,
      normw)


def _pack_w_in(w_in):
    s1 = Q_LORA_RANK
    s2 = s1 + KV_LORA_RANK
    s3 = s2 + QK_ROPE_DIM
    s4 = s3 + SSD_INNER
    s5 = s4 + CONV_DIM
    zeros = lambda n: jnp.zeros((D_MODEL, n), w_in.dtype)
    kr_dt = jnp.concatenate([zeros(ROPE_LO), w_in[:, s2:s3], w_in[:, s5:],
                             zeros(HEAD_PAD - DT_LANE0 - SSD_HEADS)], axis=1)
    return jnp.concatenate([w_in[:, :s2], kr_dt, w_in[:, s3:s5]], axis=1).astype(BF16)


def _pack_w_uq(w_uq):
    w = w_uq.reshape(Q_LORA_RANK, MLA_HEADS, QK_NOPE_DIM + QK_ROPE_DIM)
    pad = jnp.zeros((Q_LORA_RANK, MLA_HEADS, HEAD_PAD - ROPE_HI), w_uq.dtype)
    return jnp.concatenate([w, pad], axis=-1).reshape(Q_LORA_RANK, MLA_HEADS * HEAD_PAD).astype(BF16)


def _pack_w_ukv(w_ukv):
    w = w_ukv.reshape(KV_LORA_RANK, MLA_HEADS, QK_NOPE_DIM + V_HEAD_DIM)
    pad = jnp.zeros((KV_LORA_RANK, MLA_HEADS, HEAD_PAD - QK_NOPE_DIM), w_ukv.dtype)
    w_uk = jnp.concatenate([w[..., :QK_NOPE_DIM], pad], axis=-1).reshape(KV_LORA_RANK, MLA_HEADS * HEAD_PAD)
    w_uv = w[..., QK_NOPE_DIM:].reshape(KV_LORA_RANK, MLA_WIDTH)
    return w_uk.astype(BF16), w_uv.astype(BF16)


def _per_lane(v):
    return jnp.repeat(v.astype(F32), SSD_HEAD_DIM)[None, :]


def kernel(x, positions, pre_mix_norm, w_in, q_norm, w_uq, kv_norm, w_ukv, conv_w, conv_b, dt_bias, a_log,
           d_skip, ssd_norm, w_out, post_mix_norm, pre_mlp_norm, w_up, w_down, post_mlp_norm):
    batch, seq, _ = x.shape
    depth = w_in.shape[0]
    T = batch * seq
    assert seq % ATTN_ROWS == 0 and seq % PROJ_ROWS == 0 and T % MLP_ROWS == 0

    inv_freq = ROPE_THETA ** (-jnp.arange(0, QK_ROPE_DIM, 2, dtype=F32) / QK_ROPE_DIM)
    invf = jnp.concatenate([jnp.zeros((ROPE_LO,), F32), inv_freq, inv_freq,
                            jnp.zeros((HEAD_PAD - ROPE_HI,), F32)])[None, :]
    pos_b = jnp.broadcast_to(positions.reshape(T, 1), (T, LANES))
    row = lambda v: v.astype(F32)[None, :]

    h = x.reshape(T, D_MODEL)
    for l in range(depth):
        w_uk, w_uv = _pack_w_ukv(w_ukv[l])
        dt_bias_lanes = jnp.zeros((1, LANES), F32).at[0, DT_LANE0:DT_LANE0 + SSD_HEADS].set(
            dt_bias[l].astype(F32))
        q, k, vt, z, xbc, dt = _proj_call(h, pos_b, invf, row(pre_mix_norm[l]), _pack_w_in(w_in[l]),
                                         row(q_norm[l]), _pack_w_uq(w_uq[l]), row(kv_norm[l]), w_uk, w_uv)
        y_att = _attn_call(q, k, vt, batch, seq)
        h = _mlp_call(h, y_att, z, xbc, dt, w_out[l].astype(BF16), row(post_mix_norm[l]),
                      row(pre_mlp_norm[l]), w_up[l].astype(BF16), w_down[l].astype(BF16),
                      row(post_mlp_norm[l]), conv_w[l].astype(F32), row(conv_b[l]), dt_bias_lanes,
                      _per_lane(a_log[l]), _per_lane(d_skip[l]), row(ssd_norm[l]), seq)
    return h.reshape(batch, seq, D_MODEL)
```

```python
import functools
import math

import jax
import jax.numpy as jnp
from jax import lax
from jax.experimental import pallas as pl
from jax.experimental.pallas import tpu as pltpu

F32 = jnp.float32
BF16 = jnp.bfloat16

D_MODEL = 1024
CHUNK = 64
EPS = 1e-6
MLA_HEADS = 8
QK_NOPE_DIM = 64
QK_ROPE_DIM = 32
V_HEAD_DIM = 64
Q_LORA_RANK = 768
KV_LORA_RANK = 256
ROPE_THETA = 10000.0
MLA_WIDTH = MLA_HEADS * V_HEAD_DIM
SSD_HEADS = 8
SSD_HEAD_DIM = 64
SSD_INNER = SSD_HEADS * SSD_HEAD_DIM
SSD_GROUPS = 2
SSD_STATE = 128
CONV_WIDTH = 4
CONV_DIM = SSD_INNER + 2 * SSD_GROUPS * SSD_STATE
D_FF = 4 * D_MODEL

LANES = 128
SUBLANES = 8
HEAD_PAD = LANES
HALF_ROPE = QK_ROPE_DIM // 2
ROPE_LO = QK_NOPE_DIM
ROPE_MID = QK_NOPE_DIM + HALF_ROPE
ROPE_HI = QK_NOPE_DIM + QK_ROPE_DIM
HEADS_PER_GROUP = SSD_HEADS // SSD_GROUPS
GROUP_WIDTH = HEADS_PER_GROUP * SSD_HEAD_DIM

C_Q0, C_Q1 = 0, Q_LORA_RANK
C_KV0, C_KV1 = C_Q1, C_Q1 + KV_LORA_RANK
C_KR0, C_KR1 = C_KV1, C_KV1 + HEAD_PAD
C_Z0, C_Z1 = C_KR1, C_KR1 + SSD_INNER
C_X0, C_X1 = C_Z1, C_Z1 + CONV_DIM
PROJ_COLS = C_X1
DT_LANE0 = ROPE_HI

BF16_SUBLANES = 2 * SUBLANES
VT_ROWS = V_HEAD_DIM + BF16_SUBLANES

ATTN_ROWS = 512
ATTN_HEADS = 8
PROJ_ROWS = ATTN_ROWS
MLP_ROWS = PROJ_ROWS
MLP_SLAB = 2 * LANES
FF_BLOCK = 1024
VMEM_LIMIT = 56 * 1024 * 1024


def _rms(x, w):
    return x * lax.rsqrt(jnp.mean(x * x, axis=-1, keepdims=True) + EPS) * w


def _silu(x):
    half = 0.5 * x
    return half + half * jnp.tanh(half)


def _split3(x):
    hi = x.astype(BF16)
    rest = x - hi.astype(F32)
    mid = rest.astype(BF16)
    return hi, mid, (rest - mid.astype(F32)).astype(BF16)


def _const_spec(shape):
    zeros = (0,) * len(shape)
    return pl.BlockSpec(shape, lambda *_: zeros, pipeline_mode=pl.Buffered(1))


def _proj_body(h_ref, pos_ref, invf_ref, prew_ref, w1_ref, qnw_ref, wuq_ref, kvnw_ref, wuk_ref, wuv_ref,
               q_ref, k_ref, vt_ref, z_ref, xbc_ref, dt_ref):
    u = _rms(h_ref[...], prew_ref[...]).astype(BF16)

    def seg(c0, c1):
        return jnp.dot(u, w1_ref[:, c0:c1], preferred_element_type=F32)

    c_q = seg(C_Q0, C_Q1)
    c_kv = seg(C_KV0, C_KV1)
    kr_dt = seg(C_KR0, C_KR1)
    z_ref[...] = seg(C_Z0, C_Z1)
    xbc_ref[...] = seg(C_X0, C_X1)
    dt_ref[...] = kr_dt

    lane = lax.broadcasted_iota(jnp.int32, (1, LANES), 1)
    k_rope = jnp.where(lane < ROPE_HI, kr_dt, 0.0)
    first_half = (lane >= ROPE_LO) & (lane < ROPE_MID)
    second_half = (lane >= ROPE_MID) & (lane < ROPE_HI)
    ang = pos_ref[...].astype(F32) * invf_ref[...]
    cos = jnp.where(first_half | second_half, jnp.cos(ang), 1.0)
    sin = jnp.sin(ang)
    sin_first = jnp.where(first_half, -sin, 0.0)
    sin_second = jnp.where(second_half, sin, 0.0)

    def rope(t):
        return (t * cos + pltpu.roll(t, LANES - HALF_ROPE, 1) * sin_first
                + pltpu.roll(t, HALF_ROPE, 1) * sin_second)

    scale = (QK_NOPE_DIM + QK_ROPE_DIM) ** -0.5 * math.log2(math.e)
    qf = jnp.dot(_rms(c_q, qnw_ref[...]).astype(BF16), wuq_ref[...], preferred_element_type=F32)
    c_kv_n = _rms(c_kv, kvnw_ref[...]).astype(BF16)
    kf = jnp.dot(c_kv_n, wuk_ref[...], preferred_element_type=F32)
    k_rope = rope(k_rope)
    for h in range(MLA_HEADS):
        blk = slice(h * HEAD_PAD, (h + 1) * HEAD_PAD)
        q_ref[h] = (rope(qf[:, blk]) * scale).astype(BF16)
        k_ref[h] = (kf[:, blk] + k_rope).astype(BF16)
    v_t = jnp.dot(c_kv_n, wuv_ref[...], preferred_element_type=F32).T
    ones = jnp.ones((BF16_SUBLANES, v_t.shape[1]), BF16)
    for h in range(MLA_HEADS):
        vt_ref[h, 0, 0:V_HEAD_DIM, :] = v_t[h * V_HEAD_DIM:(h + 1) * V_HEAD_DIM, :].astype(BF16)
        vt_ref[h, 0, V_HEAD_DIM:VT_ROWS, :] = ones


def _proj_call(h, pos_b, invf, prew, w1, qnw, wuq, kvnw, wuk, wuv):
    T = h.shape[0]
    rows = PROJ_ROWS
    row_spec = lambda cols: pl.BlockSpec((rows, cols), lambda i: (i, 0))
    head_spec = pl.BlockSpec((MLA_HEADS, rows, HEAD_PAD), lambda i: (0, i, 0))
    return pl.pallas_call(
        _proj_body,
        grid=(T // rows,),
        in_specs=[row_spec(D_MODEL), row_spec(LANES), _const_spec((1, LANES)), _const_spec((1, D_MODEL)),
                  _const_spec((D_MODEL, PROJ_COLS)), _const_spec((1, Q_LORA_RANK)),
                  _const_spec((Q_LORA_RANK, MLA_HEADS * HEAD_PAD)), _const_spec((1, KV_LORA_RANK)),
                  _const_spec((KV_LORA_RANK, MLA_HEADS * HEAD_PAD)), _const_spec((KV_LORA_RANK, MLA_WIDTH))],
        out_specs=[head_spec, head_spec,
                   pl.BlockSpec((MLA_HEADS, 1, VT_ROWS, rows), lambda i: (0, i, 0, 0)),
                   row_spec(SSD_INNER), row_spec(CONV_DIM), row_spec(LANES)],
        out_shape=[jax.ShapeDtypeStruct((MLA_HEADS, T, HEAD_PAD), BF16),
                   jax.ShapeDtypeStruct((MLA_HEADS, T, HEAD_PAD), BF16),
                   jax.ShapeDtypeStruct((MLA_HEADS, T // rows, VT_ROWS, rows), BF16),
                   jax.ShapeDtypeStruct((T, SSD_INNER), F32),
                   jax.ShapeDtypeStruct((T, CONV_DIM), F32),
                   jax.ShapeDtypeStruct((T, LANES), F32)],
        compiler_params=pltpu.CompilerParams(dimension_semantics=("parallel",), vmem_limit_bytes=VMEM_LIMIT),
        name="proj",
    )(h, pos_b, invf, prew, w1, qnw, wuq, kvnw, wuk, wuv)


def _attn_body(q_ref, k_ref, vt_ref, bias_ref, o_ref, s0_scr, s1_scr, tmax_scr, m_scr, acc_scr, *, rows):
    qi = pl.program_id(2)
    s_scr = (s0_scr, s1_scr)

    def scores(hh, j, slot):
        start = pl.multiple_of(j * rows, rows)
        k = k_ref[hh, pl.ds(start, rows), :]
        s_t = lax.dot_general(k, q_ref[hh], (((1,), (1,)), ((), ())), preferred_element_type=F32)
        s_scr[slot][hh] = s_t
        tmax_scr[slot, hh] = jnp.max(s_t, axis=0, keepdims=True)

    def update(hh, j, slot, diagonal=False):
        if diagonal:
            half = rows // 2
            parts = ((slice(0, half), slice(0, half)), (slice(0, rows), slice(half, rows)))
        else:
            parts = ((slice(0, rows), slice(0, rows)),)
        for ks, qs in parts:
            s_t = s_scr[slot][hh, ks, qs]
            if diagonal:
                s_t = s_t + bias_ref[ks, qs]
                tile_max = jnp.max(s_t, axis=0, keepdims=True)
            else:
                tile_max = tmax_scr[slot, hh]
            m_prev = m_scr[hh, :, qs]
            m_new = jnp.maximum(m_prev, tile_max)
            p_t = jnp.exp2(s_t - m_new).astype(BF16)
            contrib = jnp.dot(vt_ref[hh, j][:, ks], p_t, preferred_element_type=F32)
            acc_scr[hh, :, qs] = acc_scr[hh, :, qs] * jnp.exp2(m_prev - m_new) + contrib
            m_scr[hh, :, qs] = m_new

    m_scr[...] = jnp.full_like(m_scr, -jnp.inf)
    acc_scr[...] = jnp.zeros_like(acc_scr)
    for hh in range(ATTN_HEADS):
        scores(hh, 0, 0)

    def step(j, slot, diagonal=False):
        for hh in range(ATTN_HEADS):
            scores(hh, j + 1, 1 - slot)
            update(hh, j, slot, diagonal)

    def tile_pair(i, carry):
        step(2 * i, 0)
        step(2 * i + 1, 1)
        return carry

    lax.fori_loop(0, qi // 2, tile_pair, 0)

    @pl.when(qi % 2 == 0)
    def _():
        for hh in range(ATTN_HEADS):
            update(hh, qi, 0, diagonal=True)

    @pl.when(qi % 2 == 1)
    def _():
        step(qi - 1, 0)
        for hh in range(ATTN_HEADS):
            update(hh, qi, 1, diagonal=True)

    out_t = [acc_scr[hh, 0:V_HEAD_DIM, :] / acc_scr[hh, V_HEAD_DIM:V_HEAD_DIM + 1, :]
             for hh in range(ATTN_HEADS)]
    o_ref[...] = jnp.concatenate(out_t, axis=0).T.astype(o_ref.dtype)


def _attn_call(q, k, vt, batch, seq):
    rows = ATTN_ROWS
    n_q = seq // rows
    groups = MLA_HEADS // ATTN_HEADS
    chunk_of = jnp.arange(rows, dtype=jnp.int32) // CHUNK
    bias = jnp.where(chunk_of[:, None] <= chunk_of[None, :], 0.0, -jnp.inf).astype(F32)
    return pl.pallas_call(
        functools.partial(_attn_body, rows=rows),
        grid=(batch, groups, n_q),
        in_specs=[pl.BlockSpec((ATTN_HEADS, rows, HEAD_PAD), lambda b, g, i: (g, b * n_q + i, 0)),
                  pl.BlockSpec((ATTN_HEADS, seq, HEAD_PAD), lambda b, g, i: (g, b, 0),
                               pipeline_mode=pl.Buffered(1)),
                  pl.BlockSpec((ATTN_HEADS, n_q, VT_ROWS, rows), lambda b, g, i: (g, b, 0, 0),
                               pipeline_mode=pl.Buffered(1)),
                  _const_spec((rows, rows))],
        out_specs=pl.BlockSpec((rows, ATTN_HEADS * V_HEAD_DIM), lambda b, g, i: (b * n_q + i, g)),
        out_shape=jax.ShapeDtypeStruct((batch * seq, MLA_WIDTH), BF16),
        scratch_shapes=[pltpu.VMEM((ATTN_HEADS, rows, rows), F32), pltpu.VMEM((ATTN_HEADS, rows, rows), F32),
                        pltpu.VMEM((2, ATTN_HEADS, 1, rows), F32), pltpu.VMEM((ATTN_HEADS, 1, rows), F32),
                        pltpu.VMEM((ATTN_HEADS, VT_ROWS, rows), F32)],
        compiler_params=pltpu.CompilerParams(dimension_semantics=("parallel", "parallel", "arbitrary"),
                                             vmem_limit_bytes=VMEM_LIMIT),
        name="attn",
    )(q, k, vt, bias)


def _ssd_conv_chunk(c, convw_ref, convb_ref, ext_scr, act_scr):
    tail = SUBLANES
    r0 = c * CHUNK
    window = ext_scr[r0:r0 + tail + CHUNK, :]
    conv = window[tail:, :] * convw_ref[CONV_WIDTH - 1:CONV_WIDTH, :]
    for back in range(1, CONV_WIDTH):
        w_row = convw_ref[CONV_WIDTH - 1 - back:CONV_WIDTH - back, :]
        conv = conv + pltpu.roll(window, back, 0)[tail:, :] * w_row
    act_scr[r0:r0 + CHUNK, :] = _silu(conv + convb_ref[...])


def _ssd_scan_stages(z_ref, dtr_ref, dtb_ref, expand_ref, alog_ref, dskip_ref, normw_ref, o_ref,
                     act_scr, dts_scr, y_scr, state_scr, rows):
    L = CHUNK
    shared = {}

    def prepare():
        dt_heads = jax.nn.softplus(dtr_ref[...] + dtb_ref[...])
        dts_scr[...] = jnp.dot(jnp.concatenate(_split3(dt_heads), axis=1), expand_ref[...],
                               preferred_element_type=F32)
        shared["a_coef"] = -jnp.exp(alog_ref[...]) * math.log2(math.e)
        row_i = lax.broadcasted_iota(jnp.int32, (L, SSD_INNER), 0)
        col_j = lax.broadcasted_iota(jnp.int32, (L, SSD_INNER), 1) % SSD_HEAD_DIM
        shared["upper"] = (row_i <= col_j).astype(F32)
        shared["lower"] = col_j <= row_i
        tril = (lax.broadcasted_iota(jnp.int32, (L, L), 1)
                <= lax.broadcasted_iota(jnp.int32, (L, L), 0)).astype(BF16)
        shared["tril3"] = jnp.concatenate([tril] * 3, axis=1)
        blk_r = lax.broadcasted_iota(jnp.int32, (GROUP_WIDTH, GROUP_WIDTH), 0) // SSD_HEAD_DIM
        blk_c = lax.broadcasted_iota(jnp.int32, (GROUP_WIDTH, GROUP_WIDTH), 1) // SSD_HEAD_DIM
        shared["same_head"] = blk_r == blk_c

    def first(c):
        r0 = c * L
        xs = act_scr[pl.ds(r0, L), 0:SSD_INNER]
        dt = dts_scr[pl.ds(r0, L), :]
        a = dt * shared["a_coef"]
        a_cs = jnp.dot(shared["tril3"], jnp.concatenate(_split3(a), axis=0), preferred_element_type=F32)
        a_cs_row = jnp.sum(a * shared["upper"], axis=0, keepdims=True)
        groups = []
        for g in range(SSD_GROUPS):
            b_off = SSD_INNER + g * SSD_STATE
            c_off = SSD_INNER + SSD_GROUPS * SSD_STATE + g * SSD_STATE
            b_f32 = act_scr[pl.ds(r0, L), b_off:b_off + SSD_STATE]
            b_g = b_f32.astype(BF16)
            c_g = act_scr[pl.ds(r0, L), c_off:c_off + SSD_STATE].astype(BF16)
            b_rep = jnp.concatenate([b_g] * HEADS_PER_GROUP, axis=0)
            cb = lax.dot_general(c_g, b_rep, (((1,), (1,)), ((), ())), preferred_element_type=F32)
            groups.append((b_f32, c_g, cb))
        shared[c] = (xs, xs * dt, a_cs, a_cs_row, groups)

    def second(c):
        r0 = c * L
        xs, xdt, a_cs, a_cs_row, groups = shared.pop(c)
        decay_in = jnp.exp2(jnp.where(shared["lower"], a_cs - a_cs_row, -jnp.inf))
        a_last = a_cs[L - 1:L, :]
        decay_to_end = jnp.exp2(a_last - a_cs)
        decay_from_start = jnp.exp2(a_cs)
        chunk_decay = jnp.exp2(a_last)
        for g in range(SSD_GROUPS):
            gl = slice(g * GROUP_WIDTH, (g + 1) * GROUP_WIDTH)
            b_f32, c_g, cb = groups[g]
            scores = (cb * decay_in[:, gl]).astype(BF16)
            x_g = xdt[:, gl]
            x_rep = jnp.concatenate([x_g] * HEADS_PER_GROUP, axis=0)
            x_diag = jnp.where(shared["same_head"], x_rep, 0.0).astype(BF16)
            y_diag = jnp.dot(scores, x_diag, preferred_element_type=F32)
            state = state_scr[g]
            y_off = jnp.dot(c_g, state.astype(BF16), preferred_element_type=F32) * decay_from_start[:, gl]
            x_dec = (x_g * decay_to_end[:, gl]).astype(BF16)
            new_state = jnp.dot(b_f32.T.astype(BF16), x_dec, preferred_element_type=F32)
            state_scr[g] = state * chunk_decay[:, gl] + new_state
            y_scr[pl.ds(r0, L), gl] = y_diag + y_off + dskip_ref[:, gl] * xs[:, gl]

    def finish():
        y = y_scr[...] * _silu(z_ref[...])
        for g in range(SSD_GROUPS):
            gl = slice(g * GROUP_WIDTH, (g + 1) * GROUP_WIDTH)
            yg = y[:, gl]
            yg = yg * lax.rsqrt(jnp.mean(yg * yg, axis=-1, keepdims=True) + EPS)
            o_ref[:, gl] = (yg * normw_ref[:, gl]).astype(o_ref.dtype)

    stages = [prepare]
    for c in range(rows // L):
        stages += [functools.partial(first, c), functools.partial(second, c)]
    return stages + [finish]


def _mlp_body(h_ref, ya_ref, z_ref, xbc_ref, dt_ref, wout_ref, postmix_ref, premlp_ref, wup_ref, wdn_ref,
              postmlp_ref, convw_ref, convb_ref, dtb_ref, expand_ref, alog_ref, dskip_ref, normw_ref,
              o_ref, ext_scr, act_scr, dts_scr, y_scr, ys_scr, state_scr, *, rows, tiles_per_seq):
    j = pl.program_id(0)

    @pl.when(j == 0)
    def _():
        ys_scr[...] = jnp.zeros_like(ys_scr)

    @pl.when(j % tiles_per_seq == 0)
    def _():
        ext_scr[0:SUBLANES, :] = jnp.zeros((SUBLANES, CONV_DIM), F32)
        state_scr[...] = jnp.zeros_like(state_scr)

    ys_prev = ys_scr[...]
    ext_scr[SUBLANES:SUBLANES + rows, :] = xbc_ref[...]
    scan = _ssd_scan_stages(z_ref, dt_ref, dtb_ref, expand_ref, alog_ref, dskip_ref, normw_ref, ys_scr,
                            act_scr, dts_scr, y_scr, state_scr, rows)
    scan[0]()
    pending = []
    for c in range(rows // CHUNK):
        pending += [functools.partial(_ssd_conv_chunk, c, convw_ref, convb_ref, ext_scr, act_scr),
                    scan[1 + 2 * c], scan[2 + 2 * c]]
    pending.append(scan[-1])
    n_stage = len(pending)
    paced_slabs = (D_FF // FF_BLOCK) * (FF_BLOCK // MLP_SLAB + D_MODEL // MLP_SLAB) - 2
    slabs_done = [0]

    def mixer_stage():
        slabs_done[0] += 1
        target = -(-n_stage * slabs_done[0] // paced_slabs)
        while pending and n_stage - len(pending) < target:
            pending.pop(0)()

    mixed = (jnp.dot(ya_ref[...], wout_ref[0:MLA_WIDTH, :], preferred_element_type=F32)
             + jnp.dot(ys_prev, wout_ref[MLA_WIDTH:, :], preferred_element_type=F32))
    h1 = h_ref[...] + _rms(mixed, postmix_ref[...])
    m = _rms(h1, premlp_ref[...]).astype(BF16)
    n_slab = FF_BLOCK // MLP_SLAB
    acc = [None] * (D_MODEL // MLP_SLAB)
    for c in range(D_FF // FF_BLOCK):
        ups = []
        for s in range(n_slab):
            c0 = c * FF_BLOCK + s * MLP_SLAB
            ups.append(jnp.dot(m, wup_ref[:, c0:c0 + MLP_SLAB], preferred_element_type=F32))
            mixer_stage()
        act = jnp.square(jnp.maximum(jnp.concatenate(ups, axis=1), 0.0)).astype(BF16)
        for s in range(len(acc)):
            part = jnp.dot(act, wdn_ref[c * FF_BLOCK:(c + 1) * FF_BLOCK, s * MLP_SLAB:(s + 1) * MLP_SLAB],
                           preferred_element_type=F32)
            acc[s] = part if acc[s] is None else acc[s] + part
            mixer_stage()
    assert not pending
    ext_scr[0:SUBLANES, :] = ext_scr[rows:rows + SUBLANES, :]
    o_ref[...] = h1 + _rms(jnp.concatenate(acc, axis=1), postmlp_ref[...])


def _mlp_call(h, ya, z, xbc, dt, wout, postmix, premlp, wup, wdn, postmlp, convw, convb, dtb, alog, dskip,
              normw, seq):
    T = h.shape[0]
    rows = MLP_ROWS
    n_tiles = T // rows
    mlp_spec = lambda cols: pl.BlockSpec((rows, cols), lambda j: (jnp.maximum(j - 1, 0), 0))
    ssd_spec = lambda cols: pl.BlockSpec((rows, cols), lambda j: (jnp.minimum(j, n_tiles - 1), 0))
    src_head = jnp.arange(LANES, dtype=jnp.int32)[:, None] - DT_LANE0
    dst_head = jnp.arange(SSD_INNER, dtype=jnp.int32)[None, :] // SSD_HEAD_DIM
    expand = jnp.tile((src_head == dst_head).astype(BF16), (3, 1))
    return pl.pallas_call(
        functools.partial(_mlp_body, rows=rows, tiles_per_seq=seq // rows),
        grid=(n_tiles + 1,),
        in_specs=[mlp_spec(D_MODEL), mlp_spec(MLA_WIDTH), ssd_spec(SSD_INNER), ssd_spec(CONV_DIM),
                  ssd_spec(LANES),
                  _const_spec((MLA_WIDTH + SSD_INNER, D_MODEL)), _const_spec((1, D_MODEL)),
                  _const_spec((1, D_MODEL)), _const_spec((D_MODEL, D_FF)), _const_spec((D_FF, D_MODEL)),
                  _const_spec((1, D_MODEL)),
                  _const_spec((CONV_WIDTH, CONV_DIM)), _const_spec((1, CONV_DIM)), _const_spec((1, LANES)),
                  _const_spec((3 * LANES, SSD_INNER)),
                  _const_spec((1, SSD_INNER)), _const_spec((1, SSD_INNER)), _const_spec((1, SSD_INNER))],
        out_specs=mlp_spec(D_MODEL),
        out_shape=jax.ShapeDtypeStruct((T, D_MODEL), F32),
        scratch_shapes=[pltpu.VMEM((rows + SUBLANES, CONV_DIM), F32), pltpu.VMEM((rows, CONV_DIM), F32),
                        pltpu.VMEM((rows, SSD_INNER), F32), pltpu.VMEM((rows, SSD_INNER), F32),
                        pltpu.VMEM((rows, SSD_INNER), BF16),
                        pltpu.VMEM((SSD_GROUPS, SSD_STATE, GROUP_WIDTH), F32)],
        compiler_params=pltpu.CompilerParams(dimension_semantics=("arbitrary",), vmem_limit_bytes=VMEM_LIMIT),
        name="mix_mlp",
    )(h, ya, z, xbc, dt, wout, postmix, premlp, wup, wdn, postmlp, convw, convb, dtb, expand, alog, dskip,
      normw)


def _pack_w_in(w_in):
    s1 = Q_LORA_RANK
    s2 = s1 + KV_LORA_RANK
    s3 = s2 + QK_ROPE_DIM
    s4 = s3 + SSD_INNER
    s5 = s4 + CONV_DIM
    zeros = lambda n: jnp.zeros((D_MODEL, n), w_in.dtype)
    kr_dt = jnp.concatenate([zeros(ROPE_LO), w_in[:, s2:s3], w_in[:, s5:],
                             zeros(HEAD_PAD - DT_LANE0 - SSD_HEADS)], axis=1)
    return jnp.concatenate([w_in[:, :s2], kr_dt, w_in[:, s3:s5]], axis=1).astype(BF16)


def _pack_w_uq(w_uq):
    w = w_uq.reshape(Q_LORA_RANK, MLA_HEADS, QK_NOPE_DIM + QK_ROPE_DIM)
    pad = jnp.zeros((Q_LORA_RANK, MLA_HEADS, HEAD_PAD - ROPE_HI), w_uq.dtype)
    return jnp.concatenate([w, pad], axis=-1).reshape(Q_LORA_RANK, MLA_HEADS * HEAD_PAD).astype(BF16)


def _pack_w_ukv(w_ukv):
    w = w_ukv.reshape(KV_LORA_RANK, MLA_HEADS, QK_NOPE_DIM + V_HEAD_DIM)
    pad = jnp.zeros((KV_LORA_RANK, MLA_HEADS, HEAD_PAD - QK_NOPE_DIM), w_ukv.dtype)
    w_uk = jnp.concatenate([w[..., :QK_NOPE_DIM], pad], axis=-1).reshape(KV_LORA_RANK, MLA_HEADS * HEAD_PAD)
    w_uv = w[..., QK_NOPE_DIM:].reshape(KV_LORA_RANK, MLA_WIDTH)
    return w_uk.astype(BF16), w_uv.astype(BF16)


def _per_lane(v):
    return jnp.repeat(v.astype(F32), SSD_HEAD_DIM)[None, :]


def kernel(x, positions, pre_mix_norm, w_in, q_norm, w_uq, kv_norm, w_ukv, conv_w, conv_b, dt_bias, a_log,
           d_skip, ssd_norm, w_out, post_mix_norm, pre_mlp_norm, w_up, w_down, post_mlp_norm):
    batch, seq, _ = x.shape
    depth = w_in.shape[0]
    T = batch * seq
    assert seq % ATTN_ROWS == 0 and seq % PROJ_ROWS == 0 and T % MLP_ROWS == 0

    inv_freq = ROPE_THETA ** (-jnp.arange(0, QK_ROPE_DIM, 2, dtype=F32) / QK_ROPE_DIM)
    invf = jnp.concatenate([jnp.zeros((ROPE_LO,), F32), inv_freq, inv_freq,
                            jnp.zeros((HEAD_PAD - ROPE_HI,), F32)])[None, :]
    pos_b = jnp.broadcast_to(positions.reshape(T, 1), (T, LANES))
    row = lambda v: v.astype(F32)[None, :]

    h = x.reshape(T, D_MODEL)
    for l in range(depth):
        w_uk, w_uv = _pack_w_ukv(w_ukv[l])
        dt_bias_lanes = jnp.zeros((1, LANES), F32).at[0, DT_LANE0:DT_LANE0 + SSD_HEADS].set(
            dt_bias[l].astype(F32))
        q, k, vt, z, xbc, dt = _proj_call(h, pos_b, invf, row(pre_mix_norm[l]), _pack_w_in(w_in[l]),
                                         row(q_norm[l]), _pack_w_uq(w_uq[l]), row(kv_norm[l]), w_uk, w_uv)
        y_att = _attn_call(q, k, vt, batch, seq)
        h = _mlp_call(h, y_att, z, xbc, dt, w_out[l].astype(BF16), row(post_mix_norm[l]),
                      row(pre_mlp_norm[l]), w_up[l].astype(BF16), w_down[l].astype(BF16),
                      row(post_mlp_norm[l]), conv_w[l].astype(F32), row(conv_b[l]), dt_bias_lanes,
                      _per_lane(a_log[l]), _per_lane(d_skip[l]), row(ssd_norm[l]), seq)
    return h.reshape(batch, seq, D_MODEL)
```
